```python
import jax, jax.numpy as jnp
from jax import lax
import numpy as np

D_MODEL = 1024
BATCH = 4
SEQ = 8192
DEPTH = 1

GDN_HEADS = 4
GDN_DK = 128
GDN_DV = 128
CONV_K = 4
GLA_HEADS = 4
GLA_DK = 64
GLA_DV = 128
GLA_RANK = 16
GLA_NORMALIZER = 16.0
CHUNK = 64
D_FF = 2816
ALPHA = (2.0 * DEPTH) ** 0.25
INIT_BETA = (8.0 * DEPTH) ** -0.25
LN_EPS = 1e-5
RMS_EPS = 1e-6

GDN_QK_W = GDN_HEADS * GDN_DK
GDN_V_W = GDN_HEADS * GDN_DV
GLA_QK_W = GLA_HEADS * GLA_DK
GLA_V_W = GLA_HEADS * GLA_DV
MIX_W = GDN_V_W + GLA_V_W
CONV_CH = 2 * GDN_QK_W + GDN_V_W
PROJ_SPLITS = (CONV_CH, GDN_V_W, GDN_HEADS, GDN_HEADS, GLA_QK_W, GLA_QK_W, GLA_V_W, GLA_V_W, GLA_RANK)
PROJ_DIM = sum(PROJ_SPLITS)

kernel_name = "hybrid_gdn_gla_macaron_deepnorm"


def _split_points(sizes):
    pts, acc = [], 0
    for s in sizes[:-1]:
        acc += s
        pts.append(acc)
    return pts


def _layer_norm(x, g, b):
    xf = x.astype(jnp.float32)
    mu = jnp.mean(xf, -1, keepdims=True)
    var = jnp.mean(jnp.square(xf - mu), -1, keepdims=True)
    return ((xf - mu) * lax.rsqrt(var + LN_EPS) * g + b).astype(x.dtype)


def _swiglu(x, w_gate, w_up, w_down):
    return (jax.nn.silu(x @ w_gate) * (x @ w_up)) @ w_down


def _causal_conv(x, w):
    return lax.conv_general_dilated(
        x, w[:, None, :].astype(x.dtype), window_strides=(1,), padding=((CONV_K - 1, 0),),
        dimension_numbers=("NWC", "WIO", "NWC"), feature_group_count=x.shape[-1])


def _l2norm(x):
    return x * lax.rsqrt(jnp.sum(jnp.square(x), -1, keepdims=True) + RMS_EPS)


def _gated_rmsnorm(o, gate, w):
    o = o * lax.rsqrt(jnp.mean(jnp.square(o), -1, keepdims=True) + RMS_EPS) * w
    return o * jax.nn.silu(gate)


def _to_chunks(t):
    b, l, h, d = t.shape
    return t.reshape(b, l // CHUNK, CHUNK, h, d).transpose(0, 3, 1, 2, 4)


def _scalars_to_chunks(t):
    b, l, h = t.shape
    return t.reshape(b, l // CHUNK, CHUNK, h).transpose(0, 3, 1, 2)


def _from_scan(o):
    n, b, h, c, d = o.shape
    return o.transpose(1, 0, 3, 2, 4).reshape(b, n * c, h, d)


def _gated_delta_rule(q, k, v, beta, g):
    dv = v.shape[-1]
    q = q * (GDN_DK ** -0.5)
    G = jnp.cumsum(g, axis=-1)
    causal = jnp.tril(jnp.ones((CHUNK, CHUNK), bool))
    strict = jnp.tril(jnp.ones((CHUNK, CHUNK), bool), -1)
    decay = jnp.exp(jnp.where(causal, G[..., :, None] - G[..., None, :], -jnp.inf))
    kk = jnp.einsum("bhncd,bhnsd->bhncs", k, k)
    a_low = jnp.where(strict, beta[..., None] * kk * decay, 0.0)
    eye = jnp.eye(CHUNK, dtype=q.dtype)
    rhs = jnp.concatenate([v * beta[..., None], k * (beta * jnp.exp(G))[..., None]], -1)
    sol = lax.linalg.triangular_solve(a_low + eye, rhs, left_side=True, lower=True,
                                      unit_diagonal=True)
    u, w = sol[..., :dv], sol[..., dv:]
    qk = jnp.einsum("bhncd,bhnsd->bhncs", q, k) * decay
    q_dec = q * jnp.exp(G)[..., None]
    k_dec = k * jnp.exp(G[..., -1:] - G)[..., None]
    g_last = jnp.exp(G[..., -1])

    def step(S, xs):
        u_c, w_c, qk_c, qd_c, kd_c, gl_c = xs
        v_new = u_c - jnp.einsum("bhcd,bhde->bhce", w_c, S)
        o = jnp.einsum("bhcd,bhde->bhce", qd_c, S) + jnp.einsum("bhcs,bhse->bhce", qk_c, v_new)
        S = S * gl_c[..., None, None] + jnp.einsum("bhcd,bhce->bhde", kd_c, v_new)
        return S, o

    xs = tuple(jnp.moveaxis(t, 2, 0) for t in (u, w, qk, q_dec, k_dec, g_last))
    b, h = q.shape[0], q.shape[1]
    S0 = jnp.zeros((b, h, GDN_DK, dv), q.dtype)
    _, o = lax.scan(step, S0, xs)
    return _from_scan(o)


def _gla(q, k, v, logg):
    dv = v.shape[-1]
    q = q * (GLA_DK ** -0.5)
    bcum = jnp.cumsum(logg, axis=-2)
    causal = jnp.tril(jnp.ones((CHUNK, CHUNK), bool))[..., None]

    def step(S, xs):
        q_c, k_c, v_c, b_c = xs
        diff = b_c[:, :, :, None, :] - b_c[:, :, None, :, :]
        dec = jnp.exp(jnp.where(causal, diff, -jnp.inf))
        att = jnp.einsum("bhik,bhjk,bhijk->bhij", q_c, k_c, dec)
        o = jnp.einsum("bhik,bhkv->bhiv", q_c * jnp.exp(b_c), S) + jnp.einsum("bhij,bhjv->bhiv", att, v_c)
        b_last = b_c[:, :, -1]
        S = S * jnp.exp(b_last)[..., None] + jnp.einsum(
            "bhjk,bhjv->bhkv", k_c * jnp.exp(b_last[:, :, None] - b_c), v_c)
        return S, o

    xs = tuple(jnp.moveaxis(t, 2, 0) for t in (q, k, v, bcum))
    b, h = q.shape[0], q.shape[1]
    S0 = jnp.zeros((b, h, GLA_DK, dv), q.dtype)
    _, o = lax.scan(step, S0, xs)
    return _from_scan(o)


def _token_mixer(h, w_in, conv_w, a_log, dt_bias, gdn_norm_w, w_gk, b_gk, gla_norm_w, w_out):
    bsz, l, _ = h.shape
    f32 = jnp.float32
    proj = h @ w_in
    qkv_a, z_a, beta_a, dec_a, q_b, k_b, v_b, g_b, lr_b = jnp.split(proj, _split_points(PROJ_SPLITS), -1)

    qkv = jax.nn.silu(_causal_conv(qkv_a, conv_w)).astype(f32)
    q_a, k_a, v_a = jnp.split(qkv, [GDN_QK_W, 2 * GDN_QK_W], -1)
    q_a = _l2norm(q_a.reshape(bsz, l, GDN_HEADS, GDN_DK))
    k_a = _l2norm(k_a.reshape(bsz, l, GDN_HEADS, GDN_DK))
    v_a = v_a.reshape(bsz, l, GDN_HEADS, GDN_DV)
    beta = jax.nn.sigmoid(beta_a.astype(f32))
    g = -jnp.exp(a_log.astype(f32)) * jax.nn.softplus(dec_a.astype(f32) + dt_bias.astype(f32))
    o_a = _gated_delta_rule(_to_chunks(q_a), _to_chunks(k_a), _to_chunks(v_a),
                            _scalars_to_chunks(beta), _scalars_to_chunks(g))
    o_a = _gated_rmsnorm(o_a, z_a.astype(f32).reshape(bsz, l, GDN_HEADS, GDN_DV),
                         gdn_norm_w.astype(f32)).reshape(bsz, l, GDN_V_W)

    logg = jax.nn.log_sigmoid((lr_b @ w_gk + b_gk).astype(f32)) / GLA_NORMALIZER
    q_b = q_b.astype(f32).reshape(bsz, l, GLA_HEADS, GLA_DK)
    k_b = k_b.astype(f32).reshape(bsz, l, GLA_HEADS, GLA_DK)
    v_b = v_b.astype(f32).reshape(bsz, l, GLA_HEADS, GLA_DV)
    logg = logg.reshape(bsz, l, GLA_HEADS, GLA_DK)
    o_b = _gla(_to_chunks(q_b), _to_chunks(k_b), _to_chunks(v_b), _to_chunks(logg))
    o_b = _gated_rmsnorm(o_b, g_b.astype(f32).reshape(bsz, l, GLA_HEADS, GLA_DV),
                         gla_norm_w.astype(f32)).reshape(bsz, l, GLA_V_W)

    mix = jnp.concatenate([o_a, o_b], -1).astype(h.dtype)
    return mix @ w_out


def setup_inputs(seed: int = 0) -> dict:
    key = jax.random.key(seed)
    ks = jax.random.split(key, 24)
    nrm = lambda k, shape, s: jax.random.normal(k, shape, jnp.float32) * s
    x = jax.random.normal(ks[0], (BATCH, SEQ, D_MODEL), jnp.float32)

    col_scale = jnp.concatenate([
        jnp.ones((2 * GDN_QK_W,)), jnp.full((GDN_V_W,), INIT_BETA), jnp.ones((GDN_V_W + 2 * GDN_HEADS,)),
        jnp.ones((2 * GLA_QK_W,)), jnp.full((GLA_V_W,), INIT_BETA), jnp.ones((GLA_V_W + GLA_RANK,))])
    w_in = nrm(ks[1], (DEPTH, D_MODEL, PROJ_DIM), D_MODEL ** -0.5) * col_scale
    conv_w = nrm(ks[2], (DEPTH, CONV_K, CONV_CH), CONV_K ** -0.5)
    a_log = jnp.log(jax.random.uniform(ks[3], (DEPTH, GDN_HEADS), jnp.float32, 1.0, 16.0))
    dt = jnp.exp(jax.random.uniform(ks[4], (DEPTH, GDN_HEADS), jnp.float32, np.log(1e-3), np.log(1e-1)))
    dt_bias = dt + jnp.log(-jnp.expm1(-dt))
    gdn_norm_w = 1.0 + nrm(ks[5], (DEPTH, GDN_DV), 0.02)
    w_gk = nrm(ks[6], (DEPTH, GLA_RANK, GLA_QK_W), GLA_RANK ** -0.5)
    b_gk = nrm(ks[7], (DEPTH, GLA_QK_W), 0.1)
    gla_norm_w = 1.0 + nrm(ks[8], (DEPTH, GLA_DV), 0.02)
    w_out = nrm(ks[9], (DEPTH, MIX_W, D_MODEL), MIX_W ** -0.5 * INIT_BETA)

    def ffn(k):
        k1, k2, k3 = jax.random.split(k, 3)
        return (nrm(k1, (DEPTH, D_MODEL, D_FF), D_MODEL ** -0.5),
                nrm(k2, (DEPTH, D_MODEL, D_FF), D_MODEL ** -0.5 * INIT_BETA),
                nrm(k3, (DEPTH, D_FF, D_MODEL), D_FF ** -0.5 * INIT_BETA))

    f1g, f1u, f1d = ffn(ks[10])
    f2g, f2u, f2d = ffn(ks[11])
    ln = lambda kg, kb: (1.0 + nrm(kg, (DEPTH, D_MODEL), 0.02), nrm(kb, (DEPTH, D_MODEL), 0.02))
    ln1_g, ln1_b = ln(ks[12], ks[13])
    ln2_g, ln2_b = ln(ks[14], ks[15])
    ln3_g, ln3_b = ln(ks[16], ks[17])
    return {"x": x,
            "ffn1_w_gate": f1g, "ffn1_w_up": f1u, "ffn1_w_down": f1d, "ln1_g": ln1_g, "ln1_b": ln1_b,
            "w_in": w_in, "conv_w": conv_w, "a_log": a_log, "dt_bias": dt_bias, "gdn_norm_w": gdn_norm_w,
            "w_gk": w_gk, "b_gk": b_gk, "gla_norm_w": gla_norm_w, "w_out": w_out,
            "ln2_g": ln2_g, "ln2_b": ln2_b,
            "ffn2_w_gate": f2g, "ffn2_w_up": f2u, "ffn2_w_down": f2d, "ln3_g": ln3_g, "ln3_b": ln3_b}


def reference(x, ffn1_w_gate, ffn1_w_up, ffn1_w_down, ln1_g, ln1_b,
              w_in, conv_w, a_log, dt_bias, gdn_norm_w, w_gk, b_gk, gla_norm_w, w_out,
              ln2_g, ln2_b, ffn2_w_gate, ffn2_w_up, ffn2_w_down, ln3_g, ln3_b):
    for i in range(DEPTH):
        x = _layer_norm(ALPHA * x + 0.5 * _swiglu(x, ffn1_w_gate[i], ffn1_w_up[i], ffn1_w_down[i]),
                        ln1_g[i], ln1_b[i])
        mix = _token_mixer(x, w_in[i], conv_w[i], a_log[i], dt_bias[i], gdn_norm_w[i],
                           w_gk[i], b_gk[i], gla_norm_w[i], w_out[i])
        x = _layer_norm(ALPHA * x + mix, ln2_g[i], ln2_b[i])
        x = _layer_norm(ALPHA * x + 0.5 * _swiglu(x, ffn2_w_gate[i], ffn2_w_up[i], ffn2_w_down[i]),
                        ln3_g[i], ln3_b[i])
    return x
```

```python
import functools

import numpy as np
import jax
import jax.numpy as jnp
from jax import lax
from jax.experimental import pallas as pl
from jax.experimental.pallas import tpu as pltpu

D_MODEL = 1024
GDN_HEADS = 4
GDN_DK = 128
GDN_DV = 128
CONV_K = 4
GLA_HEADS = 4
GLA_DK = 64
GLA_DV = 128
GLA_RANK = 16
GLA_NORMALIZER = 16.0
CHUNK = 64
D_FF = 2816
DEPTH = 1
ALPHA = (2.0 * DEPTH) ** 0.25
LN_EPS = 1e-5
RMS_EPS = 1e-6

GDN_QK_W = GDN_HEADS * GDN_DK
GDN_V_W = GDN_HEADS * GDN_DV
GLA_QK_W = GLA_HEADS * GLA_DK
GLA_V_W = GLA_HEADS * GLA_DV
CONV_CH = 2 * GDN_QK_W + GDN_V_W

LANES = 128
MXU_DIM = 256
VMEM_LIMIT_BYTES = 56 * 1024 * 1024

SMALL_W = LANES
PROJ_W = CONV_CH + GDN_V_W + 2 * GLA_QK_W + 2 * GLA_V_W + SMALL_W
OFF_QKV = 0
OFF_Z = CONV_CH
OFF_GQ = OFF_Z + GDN_V_W
OFF_GK = OFF_GQ + GLA_QK_W
OFF_GV = OFF_GK + GLA_QK_W
OFF_GG = OFF_GV + GLA_V_W
OFF_SMALL = OFF_GG + GLA_V_W

BF16 = jnp.bfloat16
F32 = jnp.float32
HIGHEST = lax.Precision.HIGHEST


def _dot(a, b):
    return jnp.dot(a.astype(BF16), b.astype(BF16), preferred_element_type=F32)


def _dot_nt(a, b):
    return lax.dot_general(a.astype(BF16), b.astype(BF16), (((1,), (1,)), ((), ())),
                           preferred_element_type=F32)


def _dot_tn(a, b):
    return lax.dot_general(a.astype(BF16), b.astype(BF16), (((0,), (0,)), ((), ())),
                           preferred_element_type=F32)


def _dot_f32(a, b):
    return jnp.dot(a, b, preferred_element_type=F32, precision=HIGHEST)


def _sigmoid(x):
    return 1.0 / (1.0 + jnp.exp(-x))


def _silu(x):
    return x * _sigmoid(x)


def _softplus(x):
    return jnp.maximum(x, 0.0) + jnp.log(1.0 + jnp.exp(-jnp.abs(x)))


def _layer_norm_rows(r, g, b):
    mu = jnp.mean(r, axis=-1, keepdims=True)
    c = r - mu
    var = jnp.mean(c * c, axis=-1, keepdims=True)
    return c * lax.rsqrt(var + LN_EPS) * g + b


def _ffn_ln_kernel(x_ref, wg_ref, wu_ref, wd_ref, g_ref, b_ref, o_ref, act_ref, *, ff_chunk):
    x = x_ref[...]
    xb = x.astype(BF16)
    d_ff = wg_ref.shape[1]
    for j in range(d_ff // ff_chunk):
        sl = slice(j * ff_chunk, (j + 1) * ff_chunk)
        gate = jnp.dot(xb, wg_ref[:, sl], preferred_element_type=F32)
        up = jnp.dot(xb, wu_ref[:, sl], preferred_element_type=F32)
        act_ref[:, sl] = (_silu(gate) * up).astype(BF16)
    y = jnp.dot(act_ref[...], wd_ref[...], preferred_element_type=F32)
    o_ref[...] = _layer_norm_rows(ALPHA * x + 0.5 * y, g_ref[...], b_ref[...])


def _ffn_ln(x2d, wg, wu, wd, ln_g, ln_b, *, tm):
    t, d = x2d.shape
    d_ff = wg.shape[1]
    const = lambda i: (0, 0)
    return pl.pallas_call(
        functools.partial(_ffn_ln_kernel, ff_chunk=MXU_DIM),
        grid=(t // tm,),
        in_specs=[
            pl.BlockSpec((tm, d), lambda i: (i, 0)),
            pl.BlockSpec((d, d_ff), const, pipeline_mode=pl.Buffered(1)),
            pl.BlockSpec((d, d_ff), const, pipeline_mode=pl.Buffered(1)),
            pl.BlockSpec((d_ff, d), const, pipeline_mode=pl.Buffered(1)),
            pl.BlockSpec((1, d), const),
            pl.BlockSpec((1, d), const),
        ],
        out_specs=pl.BlockSpec((tm, d), lambda i: (i, 0)),
        out_shape=jax.ShapeDtypeStruct((t, d), F32),
        scratch_shapes=[pltpu.VMEM((tm, d_ff), BF16)],
        compiler_params=pltpu.CompilerParams(
            dimension_semantics=("parallel",), vmem_limit_bytes=VMEM_LIMIT_BYTES),
        name="ffn_ln",
    )(x2d, wg, wu, wd, ln_g, ln_b)


def _proj_kernel(h_ref, w_ref, o_ref):
    o_ref[...] = jnp.dot(h_ref[...].astype(BF16), w_ref[...], preferred_element_type=F32)


def _proj(h2d, w, *, tm):
    t, d = h2d.shape
    n = w.shape[1]
    return pl.pallas_call(
        _proj_kernel,
        grid=(t // tm,),
        in_specs=[
            pl.BlockSpec((tm, d), lambda i: (i, 0)),
            pl.BlockSpec((d, n), lambda i: (0, 0), pipeline_mode=pl.Buffered(1)),
        ],
        out_specs=pl.BlockSpec((tm, n), lambda i: (i, 0)),
        out_shape=jax.ShapeDtypeStruct((t, n), F32),
        compiler_params=pltpu.CompilerParams(
            dimension_semantics=("parallel",), vmem_limit_bytes=VMEM_LIMIT_BYTES),
        name="proj",
    )(h2d, w)


def _iota2(shape, dim):
    return lax.broadcasted_iota(jnp.int32, shape, dim)


def _gated_rmsnorm(o, gate, w):
    ms = jnp.mean(o * o, axis=-1, keepdims=True)
    return o * lax.rsqrt(ms + RMS_EPS) * w * _silu(gate)


def _gdn_kernel(qkv_ref, z_ref, sm_ref, convw_ref, alog_ref, dtb_ref, nw_ref, tril_ref,
                o_ref, xpad_ref, s_ref, *, lb):
    c = CHUNK
    halo = 8

    @pl.when(pl.program_id(1) == 0)
    def _():
        xpad_ref[0:halo, :] = jnp.zeros((halo, CONV_CH), F32)
        s_ref[...] = jnp.zeros(s_ref.shape, F32)

    xpad_ref[halo:halo + lb, :] = qkv_ref[0]
    conv = jnp.zeros((lb, CONV_CH), F32)
    for j in range(CONV_K):
        start = halo - (CONV_K - 1) + j
        conv = conv + convw_ref[j:j + 1, :] * xpad_ref[start:start + lb, :]
    xpad_ref[0:halo, :] = xpad_ref[lb:lb + halo, :]
    qkv = _silu(conv)

    sm = sm_ref[0]
    beta_all = _sigmoid(sm)
    g_all = -jnp.exp(alog_ref[...]) * _softplus(sm + dtb_ref[...])

    row = _iota2((c, c), 0)
    col = _iota2((c, c), 1)
    causal = row >= col
    strict = row > col
    eye = (row == col).astype(F32)
    tril = tril_ref[...]
    nw = nw_ref[...]

    for ci in range(lb // c):
        rs = slice(ci * c, (ci + 1) * c)
        g_cum = _dot_f32(tril, g_all[rs, :])
        g_cum_t = jnp.transpose(jnp.concatenate([g_cum, g_cum], axis=0))[:, :c]
        beta_c = beta_all[rs, :]
        for h in range(GDN_HEADS):
            q = qkv[rs, h * GDN_DK:(h + 1) * GDN_DK]
            k = qkv[rs, GDN_QK_W + h * GDN_DK:GDN_QK_W + (h + 1) * GDN_DK]
            v = qkv[rs, 2 * GDN_QK_W + h * GDN_DV:2 * GDN_QK_W + (h + 1) * GDN_DV]
            q = q * lax.rsqrt(jnp.sum(q * q, axis=-1, keepdims=True) + RMS_EPS) * (GDN_DK ** -0.5)
            k = k * lax.rsqrt(jnp.sum(k * k, axis=-1, keepdims=True) + RMS_EPS)
            beta = beta_c[:, h:h + 1]
            gc = g_cum[:, GDN_HEADS + h:GDN_HEADS + h + 1]
            gr = g_cum_t[GDN_HEADS + h:GDN_HEADS + h + 1, :]
            g_last = gc[c - 1:c, :]
            decay = jnp.where(causal, jnp.exp(jnp.where(causal, gc - gr, 0.0)), 0.0)
            e_g = jnp.exp(gc)

            qk_kk = _dot_nt(jnp.concatenate([q, k], axis=0), k)
            qk = qk_kk[:c] * decay
            a_neg = jnp.where(strict, -(beta * qk_kk[c:] * decay), 0.0)
            n_pow = _dot(a_neg, a_neg)
            t_inv = eye + a_neg
            for _ in range(4):
                y = _dot(jnp.concatenate([n_pow, t_inv], axis=0), n_pow)
                n_pow = y[:c]
                t_inv = t_inv + y[c:]
            t_inv = t_inv + _dot(t_inv, n_pow)

            rhs = jnp.concatenate([v * beta, k * (beta * e_g)], axis=1)
            sol = _dot(t_inv, rhs)
            u = sol[:, :GDN_DV]
            w = sol[:, GDN_DV:]
            q_dec = q * e_g
            k_dec = k * jnp.exp(g_last - gc)

            s = s_ref[h]
            ws = _dot(jnp.concatenate([w, q_dec], axis=0), s)
            v_new = u - ws[:c]
            o = ws[c:] + _dot(qk, v_new)
            s_ref[h] = s * jnp.exp(g_last) + _dot_tn(k_dec, v_new)

            zg = z_ref[0, rs, h * GDN_DV:(h + 1) * GDN_DV]
            o_ref[0, rs, h * GDN_DV:(h + 1) * GDN_DV] = _gated_rmsnorm(o, zg, nw)


def _gdn(proj3d, conv_w, alog_row, dtb_row, norm_w, tril, *, lb):
    b, l, _ = proj3d.shape
    const2 = lambda bi, ti: (0, 0)
    return pl.pallas_call(
        functools.partial(_gdn_kernel, lb=lb),
        grid=(b, l // lb),
        in_specs=[
            pl.BlockSpec((1, lb, CONV_CH), lambda bi, ti: (bi, ti, OFF_QKV // CONV_CH)),
            pl.BlockSpec((1, lb, GDN_V_W), lambda bi, ti: (bi, ti, OFF_Z // GDN_V_W)),
            pl.BlockSpec((1, lb, SMALL_W), lambda bi, ti: (bi, ti, OFF_SMALL // SMALL_W)),
            pl.BlockSpec((CONV_K, CONV_CH), const2),
            pl.BlockSpec((1, SMALL_W), const2),
            pl.BlockSpec((1, SMALL_W), const2),
            pl.BlockSpec((1, GDN_DV), const2),
            pl.BlockSpec((CHUNK, CHUNK), const2),
        ],
        out_specs=pl.BlockSpec((1, lb, GDN_V_W), lambda bi, ti: (bi, ti, 0)),
        out_shape=jax.ShapeDtypeStruct((b, l, GDN_V_W), F32),
        scratch_shapes=[
            pltpu.VMEM((lb + 8, CONV_CH), F32),
            pltpu.VMEM((GDN_HEADS, GDN_DK, GDN_DV), F32),
        ],
        compiler_params=pltpu.CompilerParams(
            dimension_semantics=("parallel", "arbitrary"), vmem_limit_bytes=VMEM_LIMIT_BYTES),
        name="gdn",
    )(proj3d, proj3d, proj3d, conv_w, alog_row, dtb_row, norm_w, tril)


GLA_LEVELS = (32, 16, 8, 4, 2, 1)


def _gla_sum_matrix():
    c = CHUNK
    r = np.arange(c)[:, None]
    s = np.arange(c)[None, :]
    blocks = [(s <= r), (s > r)]
    for hs in GLA_LEVELS:
        pos = r % (2 * hs)
        m = r - pos + hs
        later = pos >= hs
        blocks.append(np.where(later, (s > m) & (s <= r), (s > r) & (s <= m)))
    return np.concatenate(blocks, axis=0).astype(np.float32)


def _gla_kernel(q_ref, k_ref, v_ref, g_ref, sm_ref, wgk_ref, bgk_ref, nw_ref, lmat_ref,
                o_ref, s_ref, *, lb):
    c = CHUNK

    @pl.when(pl.program_id(1) == 0)
    def _():
        s_ref[...] = jnp.zeros(s_ref.shape, F32)

    row = _iota2((c, c), 0)
    col = _iota2((c, c), 1)
    eye = row == col
    rowq = _iota2((c, GLA_QK_W), 0)
    lmat = lmat_ref[...]
    nw = nw_ref[...]
    scale = GLA_DK ** -0.5

    gate_pre = _dot_f32(sm_ref[0], wgk_ref[...]) + bgk_ref[...]
    logg_all = (jnp.minimum(gate_pre, 0.0) - jnp.log(1.0 + jnp.exp(-jnp.abs(gate_pre)))) / GLA_NORMALIZER

    for ci in range(lb // c):
        rs = slice(ci * c, (ci + 1) * c)
        q = q_ref[0, rs, :] * scale
        k = k_ref[0, rs, :]
        sums = _dot_f32(lmat, logg_all[rs, :])
        b_cum = sums[0:c]
        e_b = jnp.exp(b_cum)
        e_last = e_b[c - 1:c, :]
        q_in = q * e_b
        k_out = k * jnp.exp(sums[c:2 * c])
        qm, km = [], []
        for li, hs in enumerate(GLA_LEVELS):
            e_l = jnp.exp(sums[(2 + li) * c:(3 + li) * c])
            later = (rowq & hs) != 0
            qm.append(jnp.where(later, q * e_l, 0.0))
            km.append(jnp.where(later, 0.0, k * e_l))
        qk_diag = q * k

        for h in range(GLA_HEADS):
            ls = slice(h * GLA_DK, (h + 1) * GLA_DK)
            att = jnp.where(eye, jnp.sum(qk_diag[:, ls], axis=-1, keepdims=True), 0.0)
            for li, hs in enumerate(GLA_LEVELS):
                part = _dot_nt(qm[li][:, ls], km[li][:, ls])
                if 2 * hs < c:
                    part = jnp.where((row ^ col) < 2 * hs, part, 0.0)
                att = att + part
            v = v_ref[0, rs, h * GLA_DV:(h + 1) * GLA_DV]
            s_t = s_ref[h]
            o = _dot_nt(q_in[:, ls], s_t) + _dot(att, v)
            s_ref[h] = s_t * e_last[:, ls] + _dot_tn(v, k_out[:, ls])
            gg = g_ref[0, rs, h * GLA_DV:(h + 1) * GLA_DV]
            o_ref[0, rs, h * GLA_DV:(h + 1) * GLA_DV] = _gated_rmsnorm(o, gg, nw)


def _gla(proj3d, wgk_pad, bgk_row, norm_w, lmat, *, lb):
    b, l, _ = proj3d.shape
    const2 = lambda bi, ti: (0, 0)
    return pl.pallas_call(
        functools.partial(_gla_kernel, lb=lb),
        grid=(b, l // lb),
        in_specs=[
            pl.BlockSpec((1, lb, GLA_QK_W), lambda bi, ti: (bi, ti, OFF_GQ // GLA_QK_W)),
            pl.BlockSpec((1, lb, GLA_QK_W), lambda bi, ti: (bi, ti, OFF_GK // GLA_QK_W)),
            pl.BlockSpec((1, lb, GLA_V_W), lambda bi, ti: (bi, ti, OFF_GV // GLA_V_W)),
            pl.BlockSpec((1, lb, GLA_V_W), lambda bi, ti: (bi, ti, OFF_GG // GLA_V_W)),
            pl.BlockSpec((1, lb, SMALL_W), lambda bi, ti: (bi, ti, OFF_SMALL // SMALL_W)),
            pl.BlockSpec((SMALL_W, GLA_QK_W), const2),
            pl.BlockSpec((1, GLA_QK_W), const2),
            pl.BlockSpec((1, GLA_DV), const2),
            pl.BlockSpec(lmat.shape, const2),
        ],
        out_specs=pl.BlockSpec((1, lb, GLA_V_W), lambda bi, ti: (bi, ti, 0)),
        out_shape=jax.ShapeDtypeStruct((b, l, GLA_V_W), F32),
        scratch_shapes=[pltpu.VMEM((GLA_HEADS, GLA_DV, GLA_DK), F32)],
        compiler_params=pltpu.CompilerParams(
            dimension_semantics=("parallel", "arbitrary"), vmem_limit_bytes=VMEM_LIMIT_BYTES),
        name="gla",
    )(proj3d, proj3d, proj3d, proj3d, proj3d, wgk_pad, bgk_row, norm_w, lmat)


def _mix_ln_kernel(h_ref, oa_ref, ob_ref, wa_ref, wb_ref, g_ref, b_ref, o_ref):
    mix = (jnp.dot(oa_ref[...].astype(BF16), wa_ref[...], preferred_element_type=F32)
           + jnp.dot(ob_ref[...].astype(BF16), wb_ref[...], preferred_element_type=F32))
    o_ref[...] = _layer_norm_rows(ALPHA * h_ref[...] + mix, g_ref[...], b_ref[...])


def _mix_ln(h2d, oa2d, ob2d, w_a, w_b, ln_g, ln_b, *, tm):
    t, d = h2d.shape
    const = lambda i: (0, 0)
    return pl.pallas_call(
        _mix_ln_kernel,
        grid=(t // tm,),
        in_specs=[
            pl.BlockSpec((tm, d), lambda i: (i, 0)),
            pl.BlockSpec((tm, oa2d.shape[1]), lambda i: (i, 0)),
            pl.BlockSpec((tm, ob2d.shape[1]), lambda i: (i, 0)),
            pl.BlockSpec(w_a.shape, const, pipeline_mode=pl.Buffered(1)),
            pl.BlockSpec(w_b.shape, const, pipeline_mode=pl.Buffered(1)),
            pl.BlockSpec((1, d), const),
            pl.BlockSpec((1, d), const),
        ],
        out_specs=pl.BlockSpec((tm, d), lambda i: (i, 0)),
        out_shape=jax.ShapeDtypeStruct((t, d), F32),
        compiler_params=pltpu.CompilerParams(
            dimension_semantics=("parallel",), vmem_limit_bytes=VMEM_LIMIT_BYTES),
        name="mix_ln",
    )(h2d, oa2d, ob2d, w_a, w_b, ln_g, ln_b)


def _regroup_w_in(w_in):
    d = w_in.shape[0]
    o = 0
    qkv = w_in[:, o:o + CONV_CH]; o += CONV_CH
    z = w_in[:, o:o + GDN_V_W]; o += GDN_V_W
    beta = w_in[:, o:o + GDN_HEADS]; o += GDN_HEADS
    dec = w_in[:, o:o + GDN_HEADS]; o += GDN_HEADS
    gq = w_in[:, o:o + GLA_QK_W]; o += GLA_QK_W
    gk = w_in[:, o:o + GLA_QK_W]; o += GLA_QK_W
    gv = w_in[:, o:o + GLA_V_W]; o += GLA_V_W
    gg = w_in[:, o:o + GLA_V_W]; o += GLA_V_W
    lr = w_in[:, o:o + GLA_RANK]
    small = jnp.concatenate(
        [beta, dec, lr, jnp.zeros((d, SMALL_W - 2 * GDN_HEADS - GLA_RANK), w_in.dtype)], axis=1)
    return jnp.concatenate([qkv, z, gq, gk, gv, gg, small], axis=1)


def _lane_row(vals, offset, width):
    row = jnp.zeros((1, width), F32)
    return row.at[0, offset:offset + vals.shape[0]].set(vals.astype(F32))


def _layer(x, p, i, *, tm, lb):
    bsz, l, d = x.shape
    t = bsz * l
    row = lambda a: a.reshape(1, -1).astype(F32)

    h = _ffn_ln(x.reshape(t, d), p["ffn1_w_gate"][i].astype(BF16), p["ffn1_w_up"][i].astype(BF16),
                p["ffn1_w_down"][i].astype(BF16), row(p["ln1_g"][i]), row(p["ln1_b"][i]), tm=tm)

    proj = _proj(h, _regroup_w_in(p["w_in"][i]).astype(BF16), tm=tm).reshape(bsz, l, PROJ_W)

    tril = jnp.asarray(np.tril(np.ones((CHUNK, CHUNK), np.float32)))
    o_a = _gdn(proj, p["conv_w"][i].astype(F32),
               _lane_row(p["a_log"][i], GDN_HEADS, SMALL_W), _lane_row(p["dt_bias"][i], GDN_HEADS, SMALL_W),
               row(p["gdn_norm_w"][i]), tril, lb=lb)

    wgk_pad = jnp.zeros((SMALL_W, GLA_QK_W), F32).at[2 * GDN_HEADS:2 * GDN_HEADS + GLA_RANK, :].set(
        p["w_gk"][i].astype(F32))
    o_b = _gla(proj, wgk_pad, row(p["b_gk"][i]), row(p["gla_norm_w"][i]),
               jnp.asarray(_gla_sum_matrix()), lb=lb)

    w_out = p["w_out"][i].astype(BF16)
    h2 = _mix_ln(h, o_a.reshape(t, GDN_V_W), o_b.reshape(t, GLA_V_W), w_out[:GDN_V_W], w_out[GDN_V_W:],
                 row(p["ln2_g"][i]), row(p["ln2_b"][i]), tm=tm)

    out = _ffn_ln(h2, p["ffn2_w_gate"][i].astype(BF16), p["ffn2_w_up"][i].astype(BF16),
                  p["ffn2_w_down"][i].astype(BF16), row(p["ln3_g"][i]), row(p["ln3_b"][i]), tm=tm)
    return out.reshape(bsz, l, d)


def kernel(x, ffn1_w_gate, ffn1_w_up, ffn1_w_down, ln1_g, ln1_b, w_in, conv_w, a_log, dt_bias, gdn_norm_w,
           w_gk, b_gk, gla_norm_w, w_out, ln2_g, ln2_b, ffn2_w_gate, ffn2_w_up, ffn2_w_down, ln3_g, ln3_b):
    p = dict(ffn1_w_gate=ffn1_w_gate, ffn1_w_up=ffn1_w_up, ffn1_w_down=ffn1_w_down, ln1_g=ln1_g, ln1_b=ln1_b,
             w_in=w_in, conv_w=conv_w, a_log=a_log, dt_bias=dt_bias, gdn_norm_w=gdn_norm_w,
             w_gk=w_gk, b_gk=b_gk, gla_norm_w=gla_norm_w, w_out=w_out, ln2_g=ln2_g, ln2_b=ln2_b,
             ffn2_w_gate=ffn2_w_gate, ffn2_w_up=ffn2_w_up, ffn2_w_down=ffn2_w_down, ln3_g=ln3_g, ln3_b=ln3_b)
    bsz, l, _ = x.shape
    tm = min(512, bsz * l)
    lb = min(256, l)
    for i in range(ffn1_w_gate.shape[0]):
        x = _layer(x, p, i, tm=tm, lb=lb)
    return x
```

```python
import functools

import numpy as np
import jax
import jax.numpy as jnp
from jax import lax
from jax.experimental import pallas as pl
from jax.experimental.pallas import tpu as pltpu

D_MODEL = 1024
GDN_HEADS = 4
GDN_DK = 128
GDN_DV = 128
CONV_K = 4
GLA_HEADS = 4
GLA_DK = 64
GLA_DV = 128
GLA_RANK = 16
GLA_NORMALIZER = 16.0
CHUNK = 64
D_FF = 2816
DEPTH = 1
ALPHA = (2.0 * DEPTH) ** 0.25
LN_EPS = 1e-5
RMS_EPS = 1e-6

GDN_QK_W = GDN_HEADS * GDN_DK
GDN_V_W = GDN_HEADS * GDN_DV
GLA_QK_W = GLA_HEADS * GLA_DK
GLA_V_W = GLA_HEADS * GLA_DV
CONV_CH = 2 * GDN_QK_W + GDN_V_W

LANES = 128
MXU_DIM = 256
VMEM_LIMIT_BYTES = 56 * 1024 * 1024

SMALL_W = LANES
PROJ_W = CONV_CH + GDN_V_W + 2 * GLA_QK_W + 2 * GLA_V_W + SMALL_W
OFF_QKV = 0
OFF_Z = CONV_CH
OFF_GQ = OFF_Z + GDN_V_W
OFF_GK = OFF_GQ + GLA_QK_W
OFF_GV = OFF_GK + GLA_QK_W
OFF_GG = OFF_GV + GLA_V_W
OFF_SMALL = OFF_GG + GLA_V_W

BF16 = jnp.bfloat16
F32 = jnp.float32
HIGHEST = lax.Precision.HIGHEST


def _dot(a, b):
    return jnp.dot(a.astype(BF16), b.astype(BF16), preferred_element_type=F32)


def _dot_nt(a, b):
    return lax.dot_general(a.astype(BF16), b.astype(BF16), (((1,), (1,)), ((), ())),
                           preferred_element_type=F32)


def _dot_tn(a, b):
    return lax.dot_general(a.astype(BF16), b.astype(BF16), (((0,), (0,)), ((), ())),
                           preferred_element_type=F32)


def _dot_f32(a, b):
    return jnp.dot(a, b, preferred_element_type=F32, precision=HIGHEST)


def _sigmoid(x):
    return 1.0 / (1.0 + jnp.exp(-x))


def _silu(x):
    return x * _sigmoid(x)


def _softplus(x):
    return jnp.maximum(x, 0.0) + jnp.log(1.0 + jnp.exp(-jnp.abs(x)))


def _layer_norm_rows(r, g, b):
    mu = jnp.mean(r, axis=-1, keepdims=True)
    c = r - mu
    var = jnp.mean(c * c, axis=-1, keepdims=True)
    return c * lax.rsqrt(var + LN_EPS) * g + b


def _ffn_ln_kernel(x_ref, wg_ref, wu_ref, wd_ref, g_ref, b_ref, o_ref, act_ref, *, ff_chunk):
    x = x_ref[...]
    xb = x.astype(BF16)
    d_ff = wg_ref.shape[1]
    for j in range(d_ff // ff_chunk):
        sl = slice(j * ff_chunk, (j + 1) * ff_chunk)
        gate = jnp.dot(xb, wg_ref[:, sl], preferred_element_type=F32)
        up = jnp.dot(xb, wu_ref[:, sl], preferred_element_type=F32)
        act_ref[:, sl] = (_silu(gate) * up).astype(BF16)
    y = jnp.dot(act_ref[...], wd_ref[...], preferred_element_type=F32)
    o_ref[...] = _layer_norm_rows(ALPHA * x + 0.5 * y, g_ref[...], b_ref[...])


def _ffn_ln(x2d, wg, wu, wd, ln_g, ln_b, *, tm):
    t, d = x2d.shape
    d_ff = wg.shape[1]
    const = lambda i: (0, 0)
    return pl.pallas_call(
        functools.partial(_ffn_ln_kernel, ff_chunk=MXU_DIM),
        grid=(t // tm,),
        in_specs=[
            pl.BlockSpec((tm, d), lambda i: (i, 0)),
            pl.BlockSpec((d, d_ff), const, pipeline_mode=pl.Buffered(1)),
            pl.BlockSpec((d, d_ff), const, pipeline_mode=pl.Buffered(1)),
            pl.BlockSpec((d_ff, d), const, pipeline_mode=pl.Buffered(1)),
            pl.BlockSpec((1, d), const),
            pl.BlockSpec((1, d), const),
        ],
        out_specs=pl.BlockSpec((tm, d), lambda i: (i, 0)),
        out_shape=jax.ShapeDtypeStruct((t, d), F32),
        scratch_shapes=[pltpu.VMEM((tm, d_ff), BF16)],
        compiler_params=pltpu.CompilerParams(
            dimension_semantics=("parallel",), vmem_limit_bytes=VMEM_LIMIT_BYTES),
        name="ffn_ln",
    )(x2d, wg, wu, wd, ln_g, ln_b)


def _proj_kernel(h_ref, w_ref, o_ref):
    o_ref[...] = jnp.dot(h_ref[...].astype(BF16), w_ref[...], preferred_element_type=F32)


def _proj(h2d, w, *, tm):
    t, d = h2d.shape
    n = w.shape[1]
    return pl.pallas_call(
        _proj_kernel,
        grid=(t // tm,),
        in_specs=[
            pl.BlockSpec((tm, d), lambda i: (i, 0)),
            pl.BlockSpec((d, n), lambda i: (0, 0), pipeline_mode=pl.Buffered(1)),
        ],
        out_specs=pl.BlockSpec((tm, n), lambda i: (i, 0)),
        out_shape=jax.ShapeDtypeStruct((t, n), F32),
        compiler_params=pltpu.CompilerParams(
            dimension_semantics=("parallel",), vmem_limit_bytes=VMEM_LIMIT_BYTES),
        name="proj",
    )(h2d, w)


def _iota2(shape, dim):
    return lax.broadcasted_iota(jnp.int32, shape, dim)


def _gated_rmsnorm(o, gate, w):
    ms = jnp.mean(o * o, axis=-1, keepdims=True)
    return o * lax.rsqrt(ms + RMS_EPS) * w * _silu(gate)


def _gdn_kernel(qkv_ref, z_ref, sm_ref, convw_ref, alog_ref, dtb_ref, nw_ref, tril_ref,
                o_ref, xpad_ref, s_ref, *, lb):
    c = CHUNK
    halo = 8

    @pl.when(pl.program_id(1) == 0)
    def _():
        xpad_ref[0:halo, :] = jnp.zeros((halo, CONV_CH), F32)
        s_ref[...] = jnp.zeros(s_ref.shape, F32)

    xpad_ref[halo:halo + lb, :] = qkv_ref[0]
    conv = jnp.zeros((lb, CONV_CH), F32)
    for j in range(CONV_K):
        start = halo - (CONV_K - 1) + j
        conv = conv + convw_ref[j:j + 1, :] * xpad_ref[start:start + lb, :]
    xpad_ref[0:halo, :] = xpad_ref[lb:lb + halo, :]
    qkv = _silu(conv)

    sm = sm_ref[0]
    beta_all = _sigmoid(sm)
    g_all = -jnp.exp(alog_ref[...]) * _softplus(sm + dtb_ref[...])

    row = _iota2((c, c), 0)
    col = _iota2((c, c), 1)
    causal = row >= col
    strict = row > col
    eye = (row == col).astype(F32)
    tril = tril_ref[...]
    nw = nw_ref[...]

    n_chunks = lb // c
    chains = [(ci, h) for ci in range(n_chunks) for h in range(GDN_HEADS)]
    rows = lambda ci: slice(ci * c, (ci + 1) * c)

    g_cum = [_dot_f32(tril, g_all[rows(ci), :]) for ci in range(n_chunks)]
    g_cum_t = [jnp.transpose(jnp.concatenate([g, g], axis=0))[:, :c] for g in g_cum]

    qs, ks, vs, betas, gcs, decays, e_gs, g_lasts = {}, {}, {}, {}, {}, {}, {}, {}
    for ci, h in chains:
        rs = rows(ci)
        q = qkv[rs, h * GDN_DK:(h + 1) * GDN_DK]
        k = qkv[rs, GDN_QK_W + h * GDN_DK:GDN_QK_W + (h + 1) * GDN_DK]
        qs[ci, h] = q * lax.rsqrt(jnp.sum(q * q, axis=-1, keepdims=True) + RMS_EPS) * (GDN_DK ** -0.5)
        ks[ci, h] = k * lax.rsqrt(jnp.sum(k * k, axis=-1, keepdims=True) + RMS_EPS)
        vs[ci, h] = qkv[rs, 2 * GDN_QK_W + h * GDN_DV:2 * GDN_QK_W + (h + 1) * GDN_DV]
        betas[ci, h] = beta_all[rs, h:h + 1]
        gc = g_cum[ci][:, GDN_HEADS + h:GDN_HEADS + h + 1]
        gr = g_cum_t[ci][GDN_HEADS + h:GDN_HEADS + h + 1, :]
        gcs[ci, h] = gc
        g_lasts[ci, h] = gc[c - 1:c, :]
        decays[ci, h] = jnp.where(causal, jnp.exp(jnp.where(causal, gc - gr, 0.0)), 0.0)
        e_gs[ci, h] = jnp.exp(gc)

    qk_kk = {ch: _dot_nt(jnp.concatenate([qs[ch], ks[ch]], axis=0), ks[ch]) for ch in chains}
    qk = {ch: qk_kk[ch][:c] * decays[ch] for ch in chains}
    a_neg = {ch: jnp.where(strict, -(betas[ch] * qk_kk[ch][c:] * decays[ch]), 0.0) for ch in chains}
    n_pow = {ch: _dot(a_neg[ch], a_neg[ch]) for ch in chains}
    t_inv = {ch: eye + a_neg[ch] for ch in chains}
    for _ in range(4):
        y = {ch: _dot(jnp.concatenate([n_pow[ch], t_inv[ch]], axis=0), n_pow[ch]) for ch in chains}
        n_pow = {ch: y[ch][:c] for ch in chains}
        t_inv = {ch: t_inv[ch] + y[ch][c:] for ch in chains}
    corr = {ch: _dot(t_inv[ch], n_pow[ch]) for ch in chains}
    t_inv = {ch: t_inv[ch] + corr[ch] for ch in chains}

    sol = {ch: _dot(t_inv[ch], jnp.concatenate(
        [vs[ch] * betas[ch], ks[ch] * (betas[ch] * e_gs[ch])], axis=1)) for ch in chains}
    wq = {ch: jnp.concatenate([sol[ch][:, GDN_DV:], qs[ch] * e_gs[ch]], axis=0) for ch in chains}
    k_dec = {ch: ks[ch] * jnp.exp(g_lasts[ch] - gcs[ch]) for ch in chains}

    state = [s_ref[h] for h in range(GDN_HEADS)]
    for ci in range(n_chunks):
        heads = [(ci, h) for h in range(GDN_HEADS)]
        ws = {ch: _dot(wq[ch], state[ch[1]]) for ch in heads}
        v_new = {ch: sol[ch][:, :GDN_DV] - ws[ch][:c] for ch in heads}
        upd = {ch: _dot_tn(k_dec[ch], v_new[ch]) for ch in heads}
        intra = {ch: _dot(qk[ch], v_new[ch]) for ch in heads}
        for ch in heads:
            h = ch[1]
            state[h] = state[h] * jnp.exp(g_lasts[ch]) + upd[ch]
            zg = z_ref[0, rows(ci), h * GDN_DV:(h + 1) * GDN_DV]
            o_ref[0, rows(ci), h * GDN_DV:(h + 1) * GDN_DV] = _gated_rmsnorm(ws[ch][c:] + intra[ch], zg, nw)
    for h in range(GDN_HEADS):
        s_ref[h] = state[h]


def _gdn(proj3d, conv_w, alog_row, dtb_row, norm_w, tril, *, lb):
    b, l, _ = proj3d.shape
    const2 = lambda bi, ti: (0, 0)
    return pl.pallas_call(
        functools.partial(_gdn_kernel, lb=lb),
        grid=(b, l // lb),
        in_specs=[
            pl.BlockSpec((1, lb, CONV_CH), lambda bi, ti: (bi, ti, OFF_QKV // CONV_CH)),
            pl.BlockSpec((1, lb, GDN_V_W), lambda bi, ti: (bi, ti, OFF_Z // GDN_V_W)),
            pl.BlockSpec((1, lb, SMALL_W), lambda bi, ti: (bi, ti, OFF_SMALL // SMALL_W)),
            pl.BlockSpec((CONV_K, CONV_CH), const2),
            pl.BlockSpec((1, SMALL_W), const2),
            pl.BlockSpec((1, SMALL_W), const2),
            pl.BlockSpec((1, GDN_DV), const2),
            pl.BlockSpec((CHUNK, CHUNK), const2),
        ],
        out_specs=pl.BlockSpec((1, lb, GDN_V_W), lambda bi, ti: (bi, ti, 0)),
        out_shape=jax.ShapeDtypeStruct((b, l, GDN_V_W), F32),
        scratch_shapes=[
            pltpu.VMEM((lb + 8, CONV_CH), F32),
            pltpu.VMEM((GDN_HEADS, GDN_DK, GDN_DV), F32),
        ],
        compiler_params=pltpu.CompilerParams(
            dimension_semantics=("parallel", "arbitrary"), vmem_limit_bytes=VMEM_LIMIT_BYTES),
        name="gdn",
    )(proj3d, proj3d, proj3d, conv_w, alog_row, dtb_row, norm_w, tril)


GLA_LEVELS = (32, 16, 8, 4, 2, 1)


def _gla_sum_matrix():
    c = CHUNK
    r = np.arange(c)[:, None]
    s = np.arange(c)[None, :]
    blocks = [(s <= r), (s > r)]
    for hs in GLA_LEVELS:
        pos = r % (2 * hs)
        m = r - pos + hs
        later = pos >= hs
        blocks.append(np.where(later, (s > m) & (s <= r), (s > r) & (s <= m)))
    return np.concatenate(blocks, axis=0).astype(np.float32)


def _gla_kernel(q_ref, k_ref, v_ref, g_ref, sm_ref, wgk_ref, bgk_ref, nw_ref, lmat_ref,
                o_ref, s_ref, *, lb):
    c = CHUNK

    @pl.when(pl.program_id(1) == 0)
    def _():
        s_ref[...] = jnp.zeros(s_ref.shape, F32)

    row = _iota2((c, c), 0)
    col = _iota2((c, c), 1)
    eye = row == col
    rowq = _iota2((c, GLA_QK_W), 0)
    lmat = lmat_ref[...]
    nw = nw_ref[...]
    scale = GLA_DK ** -0.5

    gate_pre = _dot_f32(sm_ref[0], wgk_ref[...]) + bgk_ref[...]
    logg_all = (jnp.minimum(gate_pre, 0.0) - jnp.log(1.0 + jnp.exp(-jnp.abs(gate_pre)))) / GLA_NORMALIZER

    n_chunks = lb // c
    chains = [(ci, h) for ci in range(n_chunks) for h in range(GLA_HEADS)]
    rows = lambda ci: slice(ci * c, (ci + 1) * c)
    lanes = lambda h: slice(h * GLA_DK, (h + 1) * GLA_DK)

    sums = [_dot_f32(lmat, logg_all[rows(ci), :]) for ci in range(n_chunks)]
    q_in, k_out, e_last, qm, km, qk_diag = [], [], [], [], [], []
    for ci in range(n_chunks):
        q = q_ref[0, rows(ci), :] * scale
        k = k_ref[0, rows(ci), :]
        e_b = jnp.exp(sums[ci][0:c])
        e_last.append(e_b[c - 1:c, :])
        q_in.append(q * e_b)
        k_out.append(k * jnp.exp(sums[ci][c:2 * c]))
        qm_c, km_c = [], []
        for li, hs in enumerate(GLA_LEVELS):
            e_l = jnp.exp(sums[ci][(2 + li) * c:(3 + li) * c])
            later = (rowq & hs) != 0
            qm_c.append(jnp.where(later, q * e_l, 0.0))
            km_c.append(jnp.where(later, 0.0, k * e_l))
        qm.append(qm_c)
        km.append(km_c)
        qk_diag.append(q * k)

    att = {(ci, h): jnp.where(eye, jnp.sum(qk_diag[ci][:, lanes(h)], axis=-1, keepdims=True), 0.0)
           for ci, h in chains}
    for li, hs in enumerate(GLA_LEVELS):
        part = {(ci, h): _dot_nt(qm[ci][li][:, lanes(h)], km[ci][li][:, lanes(h)]) for ci, h in chains}
        for ch in chains:
            p = part[ch] if 2 * hs >= c else jnp.where((row ^ col) < 2 * hs, part[ch], 0.0)
            att[ch] = att[ch] + p
    vs = {(ci, h): v_ref[0, rows(ci), h * GLA_DV:(h + 1) * GLA_DV] for ci, h in chains}
    intra = {ch: _dot(att[ch], vs[ch]) for ch in chains}
    upd = {(ci, h): _dot_tn(vs[ci, h], k_out[ci][:, lanes(h)]) for ci, h in chains}

    state = [s_ref[h] for h in range(GLA_HEADS)]
    for ci in range(n_chunks):
        inter = [_dot_nt(q_in[ci][:, lanes(h)], state[h]) for h in range(GLA_HEADS)]
        for h in range(GLA_HEADS):
            state[h] = state[h] * e_last[ci][:, lanes(h)] + upd[ci, h]
            gg = g_ref[0, rows(ci), h * GLA_DV:(h + 1) * GLA_DV]
            o_ref[0, rows(ci), h * GLA_DV:(h + 1) * GLA_DV] = _gated_rmsnorm(inter[h] + intra[ci, h], gg, nw)
    for h in range(GLA_HEADS):
        s_ref[h] = state[h]


def _gla(proj3d, wgk_pad, bgk_row, norm_w, lmat, *, lb):
    b, l, _ = proj3d.shape
    const2 = lambda bi, ti: (0, 0)
    return pl.pallas_call(
        functools.partial(_gla_kernel, lb=lb),
        grid=(b, l // lb),
        in_specs=[
            pl.BlockSpec((1, lb, GLA_QK_W), lambda bi, ti: (bi, ti, OFF_GQ // GLA_QK_W)),
            pl.BlockSpec((1, lb, GLA_QK_W), lambda bi, ti: (bi, ti, OFF_GK // GLA_QK_W)),
            pl.BlockSpec((1, lb, GLA_V_W), lambda bi, ti: (bi, ti, OFF_GV // GLA_V_W)),
            pl.BlockSpec((1, lb, GLA_V_W), lambda bi, ti: (bi, ti, OFF_GG // GLA_V_W)),
            pl.BlockSpec((1, lb, SMALL_W), lambda bi, ti: (bi, ti, OFF_SMALL // SMALL_W)),
            pl.BlockSpec((SMALL_W, GLA_QK_W), const2),
            pl.BlockSpec((1, GLA_QK_W), const2),
            pl.BlockSpec((1, GLA_DV), const2),
            pl.BlockSpec(lmat.shape, const2),
        ],
        out_specs=pl.BlockSpec((1, lb, GLA_V_W), lambda bi, ti: (bi, ti, 0)),
        out_shape=jax.ShapeDtypeStruct((b, l, GLA_V_W), F32),
        scratch_shapes=[pltpu.VMEM((GLA_HEADS, GLA_DV, GLA_DK), F32)],
        compiler_params=pltpu.CompilerParams(
            dimension_semantics=("parallel", "arbitrary"), vmem_limit_bytes=VMEM_LIMIT_BYTES),
        name="gla",
    )(proj3d, proj3d, proj3d, proj3d, proj3d, wgk_pad, bgk_row, norm_w, lmat)


def _mix_ln_kernel(h_ref, oa_ref, ob_ref, wa_ref, wb_ref, g_ref, b_ref, o_ref):
    mix = (jnp.dot(oa_ref[...].astype(BF16), wa_ref[...], preferred_element_type=F32)
           + jnp.dot(ob_ref[...].astype(BF16), wb_ref[...], preferred_element_type=F32))
    o_ref[...] = _layer_norm_rows(ALPHA * h_ref[...] + mix, g_ref[...], b_ref[...])


def _mix_ln(h2d, oa2d, ob2d, w_a, w_b, ln_g, ln_b, *, tm):
    t, d = h2d.shape
    const = lambda i: (0, 0)
    return pl.pallas_call(
        _mix_ln_kernel,
        grid=(t // tm,),
        in_specs=[
            pl.BlockSpec((tm, d), lambda i: (i, 0)),
            pl.BlockSpec((tm, oa2d.shape[1]), lambda i: (i, 0)),
            pl.BlockSpec((tm, ob2d.shape[1]), lambda i: (i, 0)),
            pl.BlockSpec(w_a.shape, const, pipeline_mode=pl.Buffered(1)),
            pl.BlockSpec(w_b.shape, const, pipeline_mode=pl.Buffered(1)),
            pl.BlockSpec((1, d), const),
            pl.BlockSpec((1, d), const),
        ],
        out_specs=pl.BlockSpec((tm, d), lambda i: (i, 0)),
        out_shape=jax.ShapeDtypeStruct((t, d), F32),
        compiler_params=pltpu.CompilerParams(
            dimension_semantics=("parallel",), vmem_limit_bytes=VMEM_LIMIT_BYTES),
        name="mix_ln",
    )(h2d, oa2d, ob2d, w_a, w_b, ln_g, ln_b)


def _regroup_w_in(w_in):
    d = w_in.shape[0]
    o = 0
    qkv = w_in[:, o:o + CONV_CH]; o += CONV_CH
    z = w_in[:, o:o + GDN_V_W]; o += GDN_V_W
    beta = w_in[:, o:o + GDN_HEADS]; o += GDN_HEADS
    dec = w_in[:, o:o + GDN_HEADS]; o += GDN_HEADS
    gq = w_in[:, o:o + GLA_QK_W]; o += GLA_QK_W
    gk = w_in[:, o:o + GLA_QK_W]; o += GLA_QK_W
    gv = w_in[:, o:o + GLA_V_W]; o += GLA_V_W
    gg = w_in[:, o:o + GLA_V_W]; o += GLA_V_W
    lr = w_in[:, o:o + GLA_RANK]
    small = jnp.concatenate(
        [beta, dec, lr, jnp.zeros((d, SMALL_W - 2 * GDN_HEADS - GLA_RANK), w_in.dtype)], axis=1)
    return jnp.concatenate([qkv, z, gq, gk, gv, gg, small], axis=1)


def _lane_row(vals, offset, width):
    row = jnp.zeros((1, width), F32)
    return row.at[0, offset:offset + vals.shape[0]].set(vals.astype(F32))


def _layer(x, p, i, *, tm, lb):
    bsz, l, d = x.shape
    t = bsz * l
    row = lambda a: a.reshape(1, -1).astype(F32)

    h = _ffn_ln(x.reshape(t, d), p["ffn1_w_gate"][i].astype(BF16), p["ffn1_w_up"][i].astype(BF16),
                p["ffn1_w_down"][i].astype(BF16), row(p["ln1_g"][i]), row(p["ln1_b"][i]), tm=tm)

    proj = _proj(h, _regroup_w_in(p["w_in"][i]).astype(BF16), tm=tm).reshape(bsz, l, PROJ_W)

    tril = jnp.asarray(np.tril(np.ones((CHUNK, CHUNK), np.float32)))
    o_a = _gdn(proj, p["conv_w"][i].astype(F32),
               _lane_row(p["a_log"][i], GDN_HEADS, SMALL_W), _lane_row(p["dt_bias"][i], GDN_HEADS, SMALL_W),
               row(p["gdn_norm_w"][i]), tril, lb=lb)

    wgk_pad = jnp.zeros((SMALL_W, GLA_QK_W), F32).at[2 * GDN_HEADS:2 * GDN_HEADS + GLA_RANK, :].set(
        p["w_gk"][i].astype(F32))
    o_b = _gla(proj, wgk_pad, row(p["b_gk"][i]), row(p["gla_norm_w"][i]),
               jnp.asarray(_gla_sum_matrix()), lb=lb)

    w_out = p["w_out"][i].astype(BF16)
    h2 = _mix_ln(h, o_a.reshape(t, GDN_V_W), o_b.reshape(t, GLA_V_W), w_out[:GDN_V_W], w_out[GDN_V_W:],
                 row(p["ln2_g"][i]), row(p["ln2_b"][i]), tm=tm)

    out = _ffn_ln(h2, p["ffn2_w_gate"][i].astype(BF16), p["ffn2_w_up"][i].astype(BF16),
                  p["ffn2_w_down"][i].astype(BF16), row(p["ln3_g"][i]), row(p["ln3_b"][i]), tm=tm)
    return out.reshape(bsz, l, d)


def kernel(x, ffn1_w_gate, ffn1_w_up, ffn1_w_down, ln1_g, ln1_b, w_in, conv_w, a_log, dt_bias, gdn_norm_w,
           w_gk, b_gk, gla_norm_w, w_out, ln2_g, ln2_b, ffn2_w_gate, ffn2_w_up, ffn2_w_down, ln3_g, ln3_b):
    p = dict(ffn1_w_gate=ffn1_w_gate, ffn1_w_up=ffn1_w_up, ffn1_w_down=ffn1_w_down, ln1_g=ln1_g, ln1_b=ln1_b,
             w_in=w_in, conv_w=conv_w, a_log=a_log, dt_bias=dt_bias, gdn_norm_w=gdn_norm_w,
             w_gk=w_gk, b_gk=b_gk, gla_norm_w=gla_norm_w, w_out=w_out, ln2_g=ln2_g, ln2_b=ln2_b,
             ffn2_w_gate=ffn2_w_gate, ffn2_w_up=ffn2_w_up, ffn2_w_down=ffn2_w_down, ln3_g=ln3_g, ln3_b=ln3_b)
    bsz, l, _ = x.shape
    tm = min(512, bsz * l)
    lb = min(256, l)
    for i in range(ffn1_w_gate.shape[0]):
        x = _layer(x, p, i, tm=tm, lb=lb)
    return x
```

```python
import functools

import numpy as np
import jax
import jax.numpy as jnp
from jax import lax
from jax.experimental import pallas as pl
from jax.experimental.pallas import tpu as pltpu

D_MODEL = 1024
GDN_HEADS = 4
GDN_DK = 128
GDN_DV = 128
CONV_K = 4
GLA_HEADS = 4
GLA_DK = 64
GLA_DV = 128
GLA_RANK = 16
GLA_NORMALIZER = 16.0
CHUNK = 64
D_FF = 2816
DEPTH = 1
ALPHA = (2.0 * DEPTH) ** 0.25
LN_EPS = 1e-5
RMS_EPS = 1e-6

GDN_QK_W = GDN_HEADS * GDN_DK
GDN_V_W = GDN_HEADS * GDN_DV
GLA_QK_W = GLA_HEADS * GLA_DK
GLA_V_W = GLA_HEADS * GLA_DV
CONV_CH = 2 * GDN_QK_W + GDN_V_W

LANES = 128
MXU_DIM = 256
VMEM_LIMIT_BYTES = 56 * 1024 * 1024

SMALL_W = LANES
PROJ_W = CONV_CH + GDN_V_W + 2 * GLA_QK_W + 2 * GLA_V_W + SMALL_W
OFF_QKV = 0
OFF_Z = CONV_CH
OFF_GQ = OFF_Z + GDN_V_W
OFF_GK = OFF_GQ + GLA_QK_W
OFF_GV = OFF_GK + GLA_QK_W
OFF_GG = OFF_GV + GLA_V_W
OFF_SMALL = OFF_GG + GLA_V_W

BF16 = jnp.bfloat16
F32 = jnp.float32


def _dot(a, b):
    return jnp.dot(a.astype(BF16), b.astype(BF16), preferred_element_type=F32)


def _dot_nt(a, b):
    return lax.dot_general(a.astype(BF16), b.astype(BF16), (((1,), (1,)), ((), ())),
                           preferred_element_type=F32)


def _dot_tn(a, b):
    return lax.dot_general(a.astype(BF16), b.astype(BF16), (((0,), (0,)), ((), ())),
                           preferred_element_type=F32)


def _split_bf16(x, pieces):
    out = []
    for _ in range(pieces - 1):
        p = x.astype(BF16)
        out.append(p)
        x = x - p.astype(F32)
    out.append(x.astype(BF16))
    return out


def _sum_dot(sel3, x):
    return jnp.dot(sel3, jnp.concatenate(_split_bf16(x, 3), axis=0), preferred_element_type=F32)


def _dot_split(a, b):
    a_hi, a_lo = _split_bf16(a, 2)
    b_hi, b_lo = _split_bf16(b, 2)
    return jnp.dot(jnp.concatenate([a_hi, a_lo, a_hi], axis=1), jnp.concatenate([b_hi, b_hi, b_lo], axis=0),
                   preferred_element_type=F32)


def _sigmoid(x):
    return 1.0 / (1.0 + jnp.exp(-x))


def _silu(x):
    return x * _sigmoid(x)


def _softplus(x):
    return jnp.maximum(x, 0.0) + jnp.log(1.0 + jnp.exp(-jnp.abs(x)))


def _layer_norm_rows(r, g, b):
    mu = jnp.mean(r, axis=-1, keepdims=True)
    c = r - mu
    var = jnp.mean(c * c, axis=-1, keepdims=True)
    return c * lax.rsqrt(var + LN_EPS) * g + b


def _ffn_ln_kernel(x_ref, wg_ref, wu_ref, wd_ref, g_ref, b_ref, o_ref, act_ref, *, ff_chunk):
    x = x_ref[...]
    xb = x.astype(BF16)
    d_ff = wg_ref.shape[1]
    for j in range(d_ff // ff_chunk):
        sl = slice(j * ff_chunk, (j + 1) * ff_chunk)
        gate = jnp.dot(xb, wg_ref[:, sl], preferred_element_type=F32)
        up = jnp.dot(xb, wu_ref[:, sl], preferred_element_type=F32)
        act_ref[:, sl] = (_silu(gate) * up).astype(BF16)
    y = jnp.dot(act_ref[...], wd_ref[...], preferred_element_type=F32)
    o_ref[...] = _layer_norm_rows(ALPHA * x + 0.5 * y, g_ref[...], b_ref[...])


def _ffn_ln(x2d, wg, wu, wd, ln_g, ln_b, *, tm):
    t, d = x2d.shape
    d_ff = wg.shape[1]
    const = lambda i: (0, 0)
    return pl.pallas_call(
        functools.partial(_ffn_ln_kernel, ff_chunk=MXU_DIM),
        grid=(t // tm,),
        in_specs=[
            pl.BlockSpec((tm, d), lambda i: (i, 0)),
            pl.BlockSpec((d, d_ff), const, pipeline_mode=pl.Buffered(1)),
            pl.BlockSpec((d, d_ff), const, pipeline_mode=pl.Buffered(1)),
            pl.BlockSpec((d_ff, d), const, pipeline_mode=pl.Buffered(1)),
            pl.BlockSpec((1, d), const),
            pl.BlockSpec((1, d), const),
        ],
        out_specs=pl.BlockSpec((tm, d), lambda i: (i, 0)),
        out_shape=jax.ShapeDtypeStruct((t, d), F32),
        scratch_shapes=[pltpu.VMEM((tm, d_ff), BF16)],
        compiler_params=pltpu.CompilerParams(
            dimension_semantics=("parallel",), vmem_limit_bytes=VMEM_LIMIT_BYTES),
        name="ffn_ln",
    )(x2d, wg, wu, wd, ln_g, ln_b)


def _proj_kernel(h_ref, w_ref, o_ref):
    o_ref[...] = jnp.dot(h_ref[...].astype(BF16), w_ref[...], preferred_element_type=F32)


def _proj(h2d, w, *, tm):
    t, d = h2d.shape
    n = w.shape[1]
    return pl.pallas_call(
        _proj_kernel,
        grid=(t // tm,),
        in_specs=[
            pl.BlockSpec((tm, d), lambda i: (i, 0)),
            pl.BlockSpec((d, n), lambda i: (0, 0), pipeline_mode=pl.Buffered(1)),
        ],
        out_specs=pl.BlockSpec((tm, n), lambda i: (i, 0)),
        out_shape=jax.ShapeDtypeStruct((t, n), F32),
        compiler_params=pltpu.CompilerParams(
            dimension_semantics=("parallel",), vmem_limit_bytes=VMEM_LIMIT_BYTES),
        name="proj",
    )(h2d, w)


def _iota2(shape, dim):
    return lax.broadcasted_iota(jnp.int32, shape, dim)


def _gated_rmsnorm(o, gate, w):
    ms = jnp.mean(o * o, axis=-1, keepdims=True)
    return o * lax.rsqrt(ms + RMS_EPS) * w * _silu(gate)


def _gdn_kernel(qkv_ref, z_ref, sm_ref, convw_ref, alog_ref, dtb_ref, nw_ref, tril_ref,
                o_ref, xpad_ref, s_ref, *, lb):
    c = CHUNK
    halo = 8

    @pl.when(pl.program_id(1) == 0)
    def _():
        xpad_ref[0:halo, :] = jnp.zeros((halo, CONV_CH), F32)
        s_ref[...] = jnp.zeros(s_ref.shape, F32)

    xpad_ref[halo:halo + lb, :] = qkv_ref[0]
    ext = xpad_ref[...]
    conv = convw_ref[CONV_K - 1:CONV_K, :] * ext[halo:, :]
    for j in range(CONV_K - 1):
        shift = CONV_K - 1 - j
        conv = conv + convw_ref[j:j + 1, :] * pltpu.roll(ext, shift, 0)[halo:, :]
    xpad_ref[0:halo, :] = ext[lb:lb + halo, :]
    qkv = _silu(conv)

    def _l2n(x, scale):
        return x * (lax.rsqrt(jnp.sum(x * x, axis=-1, keepdims=True) + RMS_EPS) * scale)
    q_n = [_l2n(qkv[:, h * GDN_DK:(h + 1) * GDN_DK], GDN_DK ** -0.5) for h in range(GDN_HEADS)]
    k_n = [_l2n(qkv[:, GDN_QK_W + h * GDN_DK:GDN_QK_W + (h + 1) * GDN_DK], 1.0) for h in range(GDN_HEADS)]

    sm = sm_ref[0]
    beta_all = _sigmoid(sm)
    g_all = -jnp.exp(alog_ref[...]) * _softplus(sm + dtb_ref[...])

    row = _iota2((c, c), 0)
    col = _iota2((c, c), 1)
    causal = row >= col
    strict = row > col
    eye = (row == col).astype(F32)
    tril = tril_ref[...]
    nw = nw_ref[...]

    n_chunks = lb // c
    chains = [(ci, h) for ci in range(n_chunks) for h in range(GDN_HEADS)]
    rows = lambda ci: slice(ci * c, (ci + 1) * c)

    g_cum = [_sum_dot(tril, g_all[rows(ci), :]) for ci in range(n_chunks)]
    g_cum_t = [jnp.transpose(jnp.concatenate([g, g], axis=0))[:, :c] for g in g_cum]

    qs, ks, vs, betas, gcs, decays, e_gs, g_lasts = {}, {}, {}, {}, {}, {}, {}, {}
    for ci, h in chains:
        rs = rows(ci)
        qs[ci, h] = q_n[h][rs, :]
        ks[ci, h] = k_n[h][rs, :]
        vs[ci, h] = qkv[rs, 2 * GDN_QK_W + h * GDN_DV:2 * GDN_QK_W + (h + 1) * GDN_DV]
        betas[ci, h] = beta_all[rs, h:h + 1]
        gc = g_cum[ci][:, GDN_HEADS + h:GDN_HEADS + h + 1]
        gr = g_cum_t[ci][GDN_HEADS + h:GDN_HEADS + h + 1, :]
        gcs[ci, h] = gc
        g_lasts[ci, h] = gc[c - 1:c, :]
        decays[ci, h] = jnp.where(causal, jnp.exp(jnp.where(causal, gc - gr, 0.0)), 0.0)
        e_gs[ci, h] = jnp.exp(gc)

    qk_kk = {ch: _dot_nt(jnp.concatenate([qs[ch], ks[ch]], axis=0), ks[ch]) for ch in chains}
    qk = {ch: qk_kk[ch][:c] * decays[ch] for ch in chains}
    a_neg = {ch: jnp.where(strict, -(betas[ch] * qk_kk[ch][c:] * decays[ch]), 0.0) for ch in chains}
    n_pow = {ch: _dot(a_neg[ch], a_neg[ch]) for ch in chains}
    t_inv = {ch: eye + a_neg[ch] for ch in chains}
    for _ in range(4):
        y = {ch: _dot(jnp.concatenate([n_pow[ch], t_inv[ch]], axis=0), n_pow[ch]) for ch in chains}
        n_pow = {ch: y[ch][:c] for ch in chains}
        t_inv = {ch: t_inv[ch] + y[ch][c:] for ch in chains}
    corr = {ch: _dot(t_inv[ch], n_pow[ch]) for ch in chains}
    t_inv = {ch: t_inv[ch] + corr[ch] for ch in chains}

    sol = {ch: _dot(t_inv[ch], jnp.concatenate(
        [vs[ch] * betas[ch], ks[ch] * (betas[ch] * e_gs[ch])], axis=1)) for ch in chains}
    wq = {ch: jnp.concatenate([sol[ch][:, GDN_DV:], qs[ch] * e_gs[ch]], axis=0) for ch in chains}
    k_dec = {ch: ks[ch] * jnp.exp(g_lasts[ch] - gcs[ch]) for ch in chains}

    state = [s_ref[h] for h in range(GDN_HEADS)]
    for ci in range(n_chunks):
        heads = [(ci, h) for h in range(GDN_HEADS)]
        ws = {ch: _dot(wq[ch], state[ch[1]]) for ch in heads}
        v_new = {ch: sol[ch][:, :GDN_DV] - ws[ch][:c] for ch in heads}
        upd = {ch: _dot_tn(k_dec[ch], v_new[ch]) for ch in heads}
        intra = {ch: _dot(qk[ch], v_new[ch]) for ch in heads}
        for ch in heads:
            h = ch[1]
            state[h] = state[h] * jnp.exp(g_lasts[ch]) + upd[ch]
            zg = z_ref[0, rows(ci), h * GDN_DV:(h + 1) * GDN_DV]
            o_ref[0, rows(ci), h * GDN_DV:(h + 1) * GDN_DV] = _gated_rmsnorm(ws[ch][c:] + intra[ch], zg, nw)
    for h in range(GDN_HEADS):
        s_ref[h] = state[h]


def _gdn(proj3d, conv_w, alog_row, dtb_row, norm_w, tril, *, lb):
    b, l, _ = proj3d.shape
    const2 = lambda bi, ti: (0, 0)
    return pl.pallas_call(
        functools.partial(_gdn_kernel, lb=lb),
        grid=(b, l // lb),
        in_specs=[
            pl.BlockSpec((1, lb, CONV_CH), lambda bi, ti: (bi, ti, OFF_QKV // CONV_CH)),
            pl.BlockSpec((1, lb, GDN_V_W), lambda bi, ti: (bi, ti, OFF_Z // GDN_V_W)),
            pl.BlockSpec((1, lb, SMALL_W), lambda bi, ti: (bi, ti, OFF_SMALL // SMALL_W)),
            pl.BlockSpec((CONV_K, CONV_CH), const2),
            pl.BlockSpec((1, SMALL_W), const2),
            pl.BlockSpec((1, SMALL_W), const2),
            pl.BlockSpec((1, GDN_DV), const2),
            pl.BlockSpec((CHUNK, 3 * CHUNK), const2),
        ],
        out_specs=pl.BlockSpec((1, lb, GDN_V_W), lambda bi, ti: (bi, ti, 0)),
        out_shape=jax.ShapeDtypeStruct((b, l, GDN_V_W), F32),
        scratch_shapes=[
            pltpu.VMEM((lb + 8, CONV_CH), F32),
            pltpu.VMEM((GDN_HEADS, GDN_DK, GDN_DV), F32),
        ],
        compiler_params=pltpu.CompilerParams(
            dimension_semantics=("parallel", "arbitrary"), vmem_limit_bytes=VMEM_LIMIT_BYTES),
        name="gdn",
    )(proj3d, proj3d, proj3d, conv_w, alog_row, dtb_row, norm_w, tril)


GLA_LEVELS = (32, 16, 8, 4, 2, 1)


def _gla_sum_matrix():
    c = CHUNK
    r = np.arange(c)[:, None]
    s = np.arange(c)[None, :]
    blocks = [(s <= r), (s > r)]
    for hs in GLA_LEVELS:
        pos = r % (2 * hs)
        m = r - pos + hs
        later = pos >= hs
        blocks.append(np.where(later, (s > m) & (s <= r), (s > r) & (s <= m)))
    return np.concatenate(blocks, axis=0).astype(np.float32)


def _gla_kernel(q_ref, k_ref, v_ref, g_ref, sm_ref, wgk_ref, bgk_ref, nw_ref, lmat_ref,
                o_ref, s_ref, *, lb):
    c = CHUNK

    @pl.when(pl.program_id(1) == 0)
    def _():
        s_ref[...] = jnp.zeros(s_ref.shape, F32)

    row = _iota2((c, c), 0)
    col = _iota2((c, c), 1)
    eye = row == col
    rowq = _iota2((c, GLA_QK_W), 0)
    lmat = lmat_ref[...]
    nw = nw_ref[...]
    scale = GLA_DK ** -0.5

    gate_pre = _dot_split(sm_ref[0], wgk_ref[...]) + bgk_ref[...]
    logg_all = (jnp.minimum(gate_pre, 0.0) - jnp.log(1.0 + jnp.exp(-jnp.abs(gate_pre)))) / GLA_NORMALIZER

    n_chunks = lb // c
    chains = [(ci, h) for ci in range(n_chunks) for h in range(GLA_HEADS)]
    rows = lambda ci: slice(ci * c, (ci + 1) * c)
    lanes = lambda h: slice(h * GLA_DK, (h + 1) * GLA_DK)

    sums = [_sum_dot(lmat, logg_all[rows(ci), :]) for ci in range(n_chunks)]
    q_in, k_out, e_last, qm, km, qk_diag = [], [], [], [], [], []
    for ci in range(n_chunks):
        q = q_ref[0, rows(ci), :] * scale
        k = k_ref[0, rows(ci), :]
        e_b = jnp.exp(sums[ci][0:c])
        e_last.append(e_b[c - 1:c, :])
        q_in.append(q * e_b)
        k_out.append(k * jnp.exp(sums[ci][c:2 * c]))
        qm_c, km_c = [], []
        for li, hs in enumerate(GLA_LEVELS):
            e_l = jnp.exp(sums[ci][(2 + li) * c:(3 + li) * c])
            later = (rowq & hs) != 0
            qm_c.append(jnp.where(later, q * e_l, 0.0))
            km_c.append(jnp.where(later, 0.0, k * e_l))
        qm.append(qm_c)
        km.append(km_c)
        qk_diag.append(q * k)

    att = {(ci, h): jnp.where(eye, jnp.sum(qk_diag[ci][:, lanes(h)], axis=-1, keepdims=True), 0.0)
           for ci, h in chains}
    for li, hs in enumerate(GLA_LEVELS):
        part = {(ci, h): _dot_nt(qm[ci][li][:, lanes(h)], km[ci][li][:, lanes(h)]) for ci, h in chains}
        for ch in chains:
            p = part[ch] if 2 * hs >= c else jnp.where((row ^ col) < 2 * hs, part[ch], 0.0)
            att[ch] = att[ch] + p
    vs = {(ci, h): v_ref[0, rows(ci), h * GLA_DV:(h + 1) * GLA_DV] for ci, h in chains}
    intra = {ch: _dot(att[ch], vs[ch]) for ch in chains}
    upd = {(ci, h): _dot_tn(vs[ci, h], k_out[ci][:, lanes(h)]) for ci, h in chains}

    state = [s_ref[h] for h in range(GLA_HEADS)]
    for ci in range(n_chunks):
        inter = [_dot_nt(q_in[ci][:, lanes(h)], state[h]) for h in range(GLA_HEADS)]
        for h in range(GLA_HEADS):
            state[h] = state[h] * e_last[ci][:, lanes(h)] + upd[ci, h]
            gg = g_ref[0, rows(ci), h * GLA_DV:(h + 1) * GLA_DV]
            o_ref[0, rows(ci), h * GLA_DV:(h + 1) * GLA_DV] = _gated_rmsnorm(inter[h] + intra[ci, h], gg, nw)
    for h in range(GLA_HEADS):
        s_ref[h] = state[h]


def _gla(proj3d, wgk_pad, bgk_row, norm_w, lmat, *, lb):
    b, l, _ = proj3d.shape
    const2 = lambda bi, ti: (0, 0)
    return pl.pallas_call(
        functools.partial(_gla_kernel, lb=lb),
        grid=(b, l // lb),
        in_specs=[
            pl.BlockSpec((1, lb, GLA_QK_W), lambda bi, ti: (bi, ti, OFF_GQ // GLA_QK_W)),
            pl.BlockSpec((1, lb, GLA_QK_W), lambda bi, ti: (bi, ti, OFF_GK // GLA_QK_W)),
            pl.BlockSpec((1, lb, GLA_V_W), lambda bi, ti: (bi, ti, OFF_GV // GLA_V_W)),
            pl.BlockSpec((1, lb, GLA_V_W), lambda bi, ti: (bi, ti, OFF_GG // GLA_V_W)),
            pl.BlockSpec((1, lb, SMALL_W), lambda bi, ti: (bi, ti, OFF_SMALL // SMALL_W)),
            pl.BlockSpec((SMALL_W, GLA_QK_W), const2),
            pl.BlockSpec((1, GLA_QK_W), const2),
            pl.BlockSpec((1, GLA_DV), const2),
            pl.BlockSpec(lmat.shape, const2),
        ],
        out_specs=pl.BlockSpec((1, lb, GLA_V_W), lambda bi, ti: (bi, ti, 0)),
        out_shape=jax.ShapeDtypeStruct((b, l, GLA_V_W), F32),
        scratch_shapes=[pltpu.VMEM((GLA_HEADS, GLA_DV, GLA_DK), F32)],
        compiler_params=pltpu.CompilerParams(
            dimension_semantics=("parallel", "arbitrary"), vmem_limit_bytes=VMEM_LIMIT_BYTES),
        name="gla",
    )(proj3d, proj3d, proj3d, proj3d, proj3d, wgk_pad, bgk_row, norm_w, lmat)


def _mix_ln_kernel(h_ref, oa_ref, ob_ref, wa_ref, wb_ref, g_ref, b_ref, o_ref):
    mix = (jnp.dot(oa_ref[...].astype(BF16), wa_ref[...], preferred_element_type=F32)
           + jnp.dot(ob_ref[...].astype(BF16), wb_ref[...], preferred_element_type=F32))
    o_ref[...] = _layer_norm_rows(ALPHA * h_ref[...] + mix, g_ref[...], b_ref[...])


def _mix_ln(h2d, oa2d, ob2d, w_a, w_b, ln_g, ln_b, *, tm):
    t, d = h2d.shape
    const = lambda i: (0, 0)
    return pl.pallas_call(
        _mix_ln_kernel,
        grid=(t // tm,),
        in_specs=[
            pl.BlockSpec((tm, d), lambda i: (i, 0)),
            pl.BlockSpec((tm, oa2d.shape[1]), lambda i: (i, 0)),
            pl.BlockSpec((tm, ob2d.shape[1]), lambda i: (i, 0)),
            pl.BlockSpec(w_a.shape, const, pipeline_mode=pl.Buffered(1)),
            pl.BlockSpec(w_b.shape, const, pipeline_mode=pl.Buffered(1)),
            pl.BlockSpec((1, d), const),
            pl.BlockSpec((1, d), const),
        ],
        out_specs=pl.BlockSpec((tm, d), lambda i: (i, 0)),
        out_shape=jax.ShapeDtypeStruct((t, d), F32),
        compiler_params=pltpu.CompilerParams(
            dimension_semantics=("parallel",), vmem_limit_bytes=VMEM_LIMIT_BYTES),
        name="mix_ln",
    )(h2d, oa2d, ob2d, w_a, w_b, ln_g, ln_b)


def _regroup_w_in(w_in):
    d = w_in.shape[0]
    o = 0
    qkv = w_in[:, o:o + CONV_CH]; o += CONV_CH
    z = w_in[:, o:o + GDN_V_W]; o += GDN_V_W
    beta = w_in[:, o:o + GDN_HEADS]; o += GDN_HEADS
    dec = w_in[:, o:o + GDN_HEADS]; o += GDN_HEADS
    gq = w_in[:, o:o + GLA_QK_W]; o += GLA_QK_W
    gk = w_in[:, o:o + GLA_QK_W]; o += GLA_QK_W
    gv = w_in[:, o:o + GLA_V_W]; o += GLA_V_W
    gg = w_in[:, o:o + GLA_V_W]; o += GLA_V_W
    lr = w_in[:, o:o + GLA_RANK]
    small = jnp.concatenate(
        [beta, dec, lr, jnp.zeros((d, SMALL_W - 2 * GDN_HEADS - GLA_RANK), w_in.dtype)], axis=1)
    return jnp.concatenate([qkv, z, gq, gk, gv, gg, small], axis=1)


def _lane_row(vals, offset, width):
    row = jnp.zeros((1, width), F32)
    return row.at[0, offset:offset + vals.shape[0]].set(vals.astype(F32))


def _layer(x, p, i, *, tm, lb):
    bsz, l, d = x.shape
    t = bsz * l
    row = lambda a: a.reshape(1, -1).astype(F32)

    h = _ffn_ln(x.reshape(t, d), p["ffn1_w_gate"][i].astype(BF16), p["ffn1_w_up"][i].astype(BF16),
                p["ffn1_w_down"][i].astype(BF16), row(p["ln1_g"][i]), row(p["ln1_b"][i]), tm=tm)

    proj = _proj(h, _regroup_w_in(p["w_in"][i]).astype(BF16), tm=tm).reshape(bsz, l, PROJ_W)

    tril = jnp.asarray(np.tile(np.tril(np.ones((CHUNK, CHUNK), np.float32)), (1, 3)), BF16)
    o_a = _gdn(proj, p["conv_w"][i].astype(F32),
               _lane_row(p["a_log"][i], GDN_HEADS, SMALL_W), _lane_row(p["dt_bias"][i], GDN_HEADS, SMALL_W),
               row(p["gdn_norm_w"][i]), tril, lb=lb)

    wgk_pad = jnp.zeros((SMALL_W, GLA_QK_W), F32).at[2 * GDN_HEADS:2 * GDN_HEADS + GLA_RANK, :].set(
        p["w_gk"][i].astype(F32))
    o_b = _gla(proj, wgk_pad, row(p["b_gk"][i]), row(p["gla_norm_w"][i]),
               jnp.asarray(np.tile(_gla_sum_matrix(), (1, 3)), BF16), lb=lb)

    w_out = p["w_out"][i].astype(BF16)
    h2 = _mix_ln(h, o_a.reshape(t, GDN_V_W), o_b.reshape(t, GLA_V_W), w_out[:GDN_V_W], w_out[GDN_V_W:],
                 row(p["ln2_g"][i]), row(p["ln2_b"][i]), tm=tm)

    out = _ffn_ln(h2, p["ffn2_w_gate"][i].astype(BF16), p["ffn2_w_up"][i].astype(BF16),
                  p["ffn2_w_down"][i].astype(BF16), row(p["ln3_g"][i]), row(p["ln3_b"][i]), tm=tm)
    return out.reshape(bsz, l, d)


def kernel(x, ffn1_w_gate, ffn1_w_up, ffn1_w_down, ln1_g, ln1_b, w_in, conv_w, a_log, dt_bias, gdn_norm_w,
           w_gk, b_gk, gla_norm_w, w_out, ln2_g, ln2_b, ffn2_w_gate, ffn2_w_up, ffn2_w_down, ln3_g, ln3_b):
    p = dict(ffn1_w_gate=ffn1_w_gate, ffn1_w_up=ffn1_w_up, ffn1_w_down=ffn1_w_down, ln1_g=ln1_g, ln1_b=ln1_b,
             w_in=w_in, conv_w=conv_w, a_log=a_log, dt_bias=dt_bias, gdn_norm_w=gdn_norm_w,
             w_gk=w_gk, b_gk=b_gk, gla_norm_w=gla_norm_w, w_out=w_out, ln2_g=ln2_g, ln2_b=ln2_b,
             ffn2_w_gate=ffn2_w_gate, ffn2_w_up=ffn2_w_up, ffn2_w_down=ffn2_w_down, ln3_g=ln3_g, ln3_b=ln3_b)
    bsz, l, _ = x.shape
    tm = min(512, bsz * l)
    lb = min(256, l)
    for i in range(ffn1_w_gate.shape[0]):
        x = _layer(x, p, i, tm=tm, lb=lb)
    return x
```

```python
import functools

import numpy as np
import jax
import jax.numpy as jnp
from jax import lax
from jax.experimental import pallas as pl
from jax.experimental.pallas import tpu as pltpu

D_MODEL = 1024
GDN_HEADS = 4
GDN_DK = 128
GDN_DV = 128
CONV_K = 4
GLA_HEADS = 4
GLA_DK = 64
GLA_DV = 128
GLA_RANK = 16
GLA_NORMALIZER = 16.0
CHUNK = 64
D_FF = 2816
DEPTH = 1
ALPHA = (2.0 * DEPTH) ** 0.25
LN_EPS = 1e-5
RMS_EPS = 1e-6

GDN_QK_W = GDN_HEADS * GDN_DK
GDN_V_W = GDN_HEADS * GDN_DV
GLA_QK_W = GLA_HEADS * GLA_DK
GLA_V_W = GLA_HEADS * GLA_DV
CONV_CH = 2 * GDN_QK_W + GDN_V_W

LANES = 128
MXU_DIM = 256
VMEM_LIMIT_BYTES = 56 * 1024 * 1024

SMALL_W = LANES
PROJ_W = CONV_CH + GDN_V_W + 2 * GLA_QK_W + 2 * GLA_V_W + SMALL_W
OFF_QKV = 0
OFF_Z = CONV_CH
OFF_GQ = OFF_Z + GDN_V_W
OFF_GK = OFF_GQ + GLA_QK_W
OFF_GV = OFF_GK + GLA_QK_W
OFF_GG = OFF_GV + GLA_V_W
OFF_SMALL = OFF_GG + GLA_V_W

BF16 = jnp.bfloat16
F32 = jnp.float32


def _dot(a, b):
    return jnp.dot(a.astype(BF16), b.astype(BF16), preferred_element_type=F32)


def _dot_nt(a, b):
    return lax.dot_general(a.astype(BF16), b.astype(BF16), (((1,), (1,)), ((), ())),
                           preferred_element_type=F32)


def _dot_tn(a, b):
    return lax.dot_general(a.astype(BF16), b.astype(BF16), (((0,), (0,)), ((), ())),
                           preferred_element_type=F32)


def _split_bf16(x, pieces):
    out = []
    for _ in range(pieces - 1):
        p = x.astype(BF16)
        out.append(p)
        x = x - p.astype(F32)
    out.append(x.astype(BF16))
    return out


def _sum_dot(sel3, x):
    return jnp.dot(sel3, jnp.concatenate(_split_bf16(x, 3), axis=0), preferred_element_type=F32)


def _dot_split(a, b):
    a_hi, a_lo = _split_bf16(a, 2)
    b_hi, b_lo = _split_bf16(b, 2)
    return jnp.dot(jnp.concatenate([a_hi, a_lo, a_hi], axis=1), jnp.concatenate([b_hi, b_hi, b_lo], axis=0),
                   preferred_element_type=F32)


def _sigmoid(x):
    return 1.0 / (1.0 + jnp.exp(-x))


def _silu(x):
    return x * _sigmoid(x)


def _softplus(x):
    return jnp.maximum(x, 0.0) + jnp.log(1.0 + jnp.exp(-jnp.abs(x)))


def _layer_norm_rows(r, g, b):
    mu = jnp.mean(r, axis=-1, keepdims=True)
    c = r - mu
    var = jnp.mean(c * c, axis=-1, keepdims=True)
    return c * lax.rsqrt(var + LN_EPS) * g + b


def _swiglu_ln(x, wg_ref, wu_ref, wd_ref, g_ref, b_ref, act_ref, ff_chunk):
    xb = x.astype(BF16)
    d_ff = wg_ref.shape[1]
    for j in range(d_ff // ff_chunk):
        sl = slice(j * ff_chunk, (j + 1) * ff_chunk)
        gate = jnp.dot(xb, wg_ref[:, sl], preferred_element_type=F32)
        up = jnp.dot(xb, wu_ref[:, sl], preferred_element_type=F32)
        act_ref[:, sl] = (_silu(gate) * up).astype(BF16)
    y = jnp.dot(act_ref[...], wd_ref[...], preferred_element_type=F32)
    return _layer_norm_rows(ALPHA * x + 0.5 * y, g_ref[...], b_ref[...])


def _ffn_ln_kernel(x_ref, wg_ref, wu_ref, wd_ref, g_ref, b_ref, o_ref, act_ref, *, ff_chunk):
    o_ref[...] = _swiglu_ln(x_ref[...], wg_ref, wu_ref, wd_ref, g_ref, b_ref, act_ref, ff_chunk)


def _mix_ffn_ln_kernel(h_ref, oa_ref, ob_ref, wa_ref, wb_ref, g2_ref, b2_ref,
                       wg_ref, wu_ref, wd_ref, g3_ref, b3_ref, o_ref, act_ref, *, ff_chunk):
    mix = (jnp.dot(oa_ref[...].astype(BF16), wa_ref[...], preferred_element_type=F32)
           + jnp.dot(ob_ref[...].astype(BF16), wb_ref[...], preferred_element_type=F32))
    x = _layer_norm_rows(ALPHA * h_ref[...] + mix, g2_ref[...], b2_ref[...])
    o_ref[...] = _swiglu_ln(x, wg_ref, wu_ref, wd_ref, g3_ref, b3_ref, act_ref, ff_chunk)


def _mix_ffn_ln(h2d, oa2d, ob2d, w_a, w_b, ln2_g, ln2_b, wg, wu, wd, ln3_g, ln3_b, *, tm):
    t, d = h2d.shape
    d_ff = wg.shape[1]
    const = lambda i: (0, 0)
    tile = lambda w: pl.BlockSpec((tm, w), lambda i: (i, 0))
    resident = lambda a: pl.BlockSpec(a.shape, const, pipeline_mode=pl.Buffered(1))
    return pl.pallas_call(
        functools.partial(_mix_ffn_ln_kernel, ff_chunk=MXU_DIM),
        grid=(t // tm,),
        in_specs=[tile(d), tile(oa2d.shape[1]), tile(ob2d.shape[1]), resident(w_a), resident(w_b),
                  pl.BlockSpec((1, d), const), pl.BlockSpec((1, d), const),
                  resident(wg), resident(wu), resident(wd),
                  pl.BlockSpec((1, d), const), pl.BlockSpec((1, d), const)],
        out_specs=tile(d),
        out_shape=jax.ShapeDtypeStruct((t, d), F32),
        scratch_shapes=[pltpu.VMEM((tm, d_ff), BF16)],
        compiler_params=pltpu.CompilerParams(
            dimension_semantics=("parallel",), vmem_limit_bytes=VMEM_LIMIT_BYTES),
        name="mix_ffn_ln",
    )(h2d, oa2d, ob2d, w_a, w_b, ln2_g, ln2_b, wg, wu, wd, ln3_g, ln3_b)


def _ffn_ln(x2d, wg, wu, wd, ln_g, ln_b, *, tm):
    t, d = x2d.shape
    d_ff = wg.shape[1]
    const = lambda i: (0, 0)
    return pl.pallas_call(
        functools.partial(_ffn_ln_kernel, ff_chunk=MXU_DIM),
        grid=(t // tm,),
        in_specs=[
            pl.BlockSpec((tm, d), lambda i: (i, 0)),
            pl.BlockSpec((d, d_ff), const, pipeline_mode=pl.Buffered(1)),
            pl.BlockSpec((d, d_ff), const, pipeline_mode=pl.Buffered(1)),
            pl.BlockSpec((d_ff, d), const, pipeline_mode=pl.Buffered(1)),
            pl.BlockSpec((1, d), const),
            pl.BlockSpec((1, d), const),
        ],
        out_specs=pl.BlockSpec((tm, d), lambda i: (i, 0)),
        out_shape=jax.ShapeDtypeStruct((t, d), F32),
        scratch_shapes=[pltpu.VMEM((tm, d_ff), BF16)],
        compiler_params=pltpu.CompilerParams(
            dimension_semantics=("parallel",), vmem_limit_bytes=VMEM_LIMIT_BYTES),
        name="ffn_ln",
    )(x2d, wg, wu, wd, ln_g, ln_b)


def _proj_kernel(h_ref, w_ref, o_ref):
    o_ref[...] = jnp.dot(h_ref[...].astype(BF16), w_ref[...], preferred_element_type=F32)


def _proj(h2d, w, *, tm):
    t, d = h2d.shape
    n = w.shape[1]
    return pl.pallas_call(
        _proj_kernel,
        grid=(t // tm,),
        in_specs=[
            pl.BlockSpec((tm, d), lambda i: (i, 0)),
            pl.BlockSpec((d, n), lambda i: (0, 0), pipeline_mode=pl.Buffered(1)),
        ],
        out_specs=pl.BlockSpec((tm, n), lambda i: (i, 0)),
        out_shape=jax.ShapeDtypeStruct((t, n), F32),
        compiler_params=pltpu.CompilerParams(
            dimension_semantics=("parallel",), vmem_limit_bytes=VMEM_LIMIT_BYTES),
        name="proj",
    )(h2d, w)


def _iota2(shape, dim):
    return lax.broadcasted_iota(jnp.int32, shape, dim)


def _gated_rmsnorm(o, gate, w):
    ms = jnp.mean(o * o, axis=-1, keepdims=True)
    return o * lax.rsqrt(ms + RMS_EPS) * w * _silu(gate)


def _gdn_kernel(qkv_ref, z_ref, sm_ref, convw_ref, alog_ref, dtb_ref, nw_ref, tril_ref,
                o_ref, xpad_ref, s_ref, wq_ref, u_ref, kd_ref, qk_ref, dl_ref, *, lb):
    c = CHUNK
    halo = 8
    n_chunks = lb // c
    chains = [(ci, h) for ci in range(n_chunks) for h in range(GDN_HEADS)]
    rows = lambda ci: slice(ci * c, (ci + 1) * c)
    nw = nw_ref[...]

    @pl.when(pl.program_id(1) == 0)
    def _():
        xpad_ref[0:halo, :] = jnp.zeros((halo, CONV_CH), F32)
        for ref in (s_ref, wq_ref, u_ref, kd_ref, qk_ref, dl_ref):
            ref[...] = jnp.zeros(ref.shape, ref.dtype)

    state = [s_ref[h] for h in range(GDN_HEADS)]
    for ci in range(n_chunks):
        heads = [(ci, h) for h in range(GDN_HEADS)]
        ws = {(ci, h): _dot(wq_ref[ci * GDN_HEADS + h], state[h]) for ci, h in heads}
        v_new = {(ci, h): u_ref[ci * GDN_HEADS + h] - ws[ci, h][:c] for ci, h in heads}
        upd = {(ci, h): _dot_tn(kd_ref[ci * GDN_HEADS + h], v_new[ci, h]) for ci, h in heads}
        intra = {(ci, h): _dot(qk_ref[ci * GDN_HEADS + h], v_new[ci, h]) for ci, h in heads}
        for ch in heads:
            h = ch[1]
            state[h] = state[h] * dl_ref[ci * GDN_HEADS + h] + upd[ch]
            zg = z_ref[0, rows(ci), h * GDN_DV:(h + 1) * GDN_DV]
            o_ref[0, rows(ci), h * GDN_DV:(h + 1) * GDN_DV] = _gated_rmsnorm(ws[ch][c:] + intra[ch], zg, nw)
    for h in range(GDN_HEADS):
        s_ref[h] = state[h]

    xpad_ref[halo:halo + lb, :] = qkv_ref[0]
    ext = xpad_ref[...]
    conv = convw_ref[CONV_K - 1:CONV_K, :] * ext[halo:, :]
    for j in range(CONV_K - 1):
        shift = CONV_K - 1 - j
        conv = conv + convw_ref[j:j + 1, :] * pltpu.roll(ext, shift, 0)[halo:, :]
    xpad_ref[0:halo, :] = ext[lb:lb + halo, :]
    qkv = _silu(conv)

    def _l2n(x, scale):
        return x * (lax.rsqrt(jnp.sum(x * x, axis=-1, keepdims=True) + RMS_EPS) * scale)
    q_n = [_l2n(qkv[:, h * GDN_DK:(h + 1) * GDN_DK], GDN_DK ** -0.5) for h in range(GDN_HEADS)]
    k_n = [_l2n(qkv[:, GDN_QK_W + h * GDN_DK:GDN_QK_W + (h + 1) * GDN_DK], 1.0) for h in range(GDN_HEADS)]

    sm = sm_ref[0]
    beta_all = _sigmoid(sm)
    g_all = -jnp.exp(alog_ref[...]) * _softplus(sm + dtb_ref[...])

    row = _iota2((c, c), 0)
    col = _iota2((c, c), 1)
    causal = row >= col
    strict = row > col
    eye = (row == col).astype(F32)
    tril = tril_ref[...]


    g_cum = [_sum_dot(tril, g_all[rows(ci), :]) for ci in range(n_chunks)]
    g_cum_t = [jnp.transpose(jnp.concatenate([g, g], axis=0))[:, :c] for g in g_cum]

    qs, ks, vs, betas, gcs, decays, e_gs, g_lasts = {}, {}, {}, {}, {}, {}, {}, {}
    for ci, h in chains:
        rs = rows(ci)
        qs[ci, h] = q_n[h][rs, :]
        ks[ci, h] = k_n[h][rs, :]
        vs[ci, h] = qkv[rs, 2 * GDN_QK_W + h * GDN_DV:2 * GDN_QK_W + (h + 1) * GDN_DV]
        betas[ci, h] = beta_all[rs, h:h + 1]
        gc = g_cum[ci][:, GDN_HEADS + h:GDN_HEADS + h + 1]
        gr = g_cum_t[ci][GDN_HEADS + h:GDN_HEADS + h + 1, :]
        gcs[ci, h] = gc
        g_lasts[ci, h] = gc[c - 1:c, :]
        decays[ci, h] = jnp.where(causal, jnp.exp(jnp.where(causal, gc - gr, 0.0)), 0.0)
        e_gs[ci, h] = jnp.exp(gc)

    qk_kk = {ch: _dot_nt(jnp.concatenate([qs[ch], ks[ch]], axis=0), ks[ch]) for ch in chains}
    qk = {ch: qk_kk[ch][:c] * decays[ch] for ch in chains}
    a_neg = {ch: jnp.where(strict, -(betas[ch] * qk_kk[ch][c:] * decays[ch]), 0.0) for ch in chains}
    n_pow = {ch: _dot(a_neg[ch], a_neg[ch]) for ch in chains}
    t_inv = {ch: eye + a_neg[ch] for ch in chains}
    for _ in range(4):
        y = {ch: _dot(jnp.concatenate([n_pow[ch], t_inv[ch]], axis=0), n_pow[ch]) for ch in chains}
        n_pow = {ch: y[ch][:c] for ch in chains}
        t_inv = {ch: t_inv[ch] + y[ch][c:] for ch in chains}
    corr = {ch: _dot(t_inv[ch], n_pow[ch]) for ch in chains}
    t_inv = {ch: t_inv[ch] + corr[ch] for ch in chains}

    sol = {ch: _dot(t_inv[ch], jnp.concatenate(
        [vs[ch] * betas[ch], ks[ch] * (betas[ch] * e_gs[ch])], axis=1)) for ch in chains}
    wq = {ch: jnp.concatenate([sol[ch][:, GDN_DV:], qs[ch] * e_gs[ch]], axis=0) for ch in chains}
    k_dec = {ch: ks[ch] * jnp.exp(g_lasts[ch] - gcs[ch]) for ch in chains}

    for ci, h in chains:
        i = ci * GDN_HEADS + h
        wq_ref[i] = wq[ci, h].astype(BF16)
        u_ref[i] = sol[ci, h][:, :GDN_DV]
        kd_ref[i] = k_dec[ci, h].astype(BF16)
        qk_ref[i] = qk[ci, h].astype(BF16)
        dl_ref[i] = jnp.broadcast_to(jnp.exp(g_lasts[ci, h]), (1, GDN_DV))


def _gdn(proj3d, conv_w, alog_row, dtb_row, norm_w, tril, *, lb):
    b, l, _ = proj3d.shape
    nb = l // lb
    n_chains = (lb // CHUNK) * GDN_HEADS
    const2 = lambda bi, ti: (0, 0)
    prep = lambda ti: jnp.minimum(ti, nb - 1)
    scan = lambda ti: jnp.maximum(ti - 1, 0)
    return pl.pallas_call(
        functools.partial(_gdn_kernel, lb=lb),
        grid=(b, nb + 1),
        in_specs=[
            pl.BlockSpec((1, lb, CONV_CH), lambda bi, ti: (bi, prep(ti), OFF_QKV // CONV_CH)),
            pl.BlockSpec((1, lb, GDN_V_W), lambda bi, ti: (bi, scan(ti), OFF_Z // GDN_V_W)),
            pl.BlockSpec((1, lb, SMALL_W), lambda bi, ti: (bi, prep(ti), OFF_SMALL // SMALL_W)),
            pl.BlockSpec((CONV_K, CONV_CH), const2),
            pl.BlockSpec((1, SMALL_W), const2),
            pl.BlockSpec((1, SMALL_W), const2),
            pl.BlockSpec((1, GDN_DV), const2),
            pl.BlockSpec((CHUNK, 3 * CHUNK), const2),
        ],
        out_specs=pl.BlockSpec((1, lb, GDN_V_W), lambda bi, ti: (bi, scan(ti), 0)),
        out_shape=jax.ShapeDtypeStruct((b, l, GDN_V_W), F32),
        scratch_shapes=[
            pltpu.VMEM((lb + 8, CONV_CH), F32),
            pltpu.VMEM((GDN_HEADS, GDN_DK, GDN_DV), F32),
            pltpu.VMEM((n_chains, 2 * CHUNK, GDN_DK), BF16),
            pltpu.VMEM((n_chains, CHUNK, GDN_DV), F32),
            pltpu.VMEM((n_chains, CHUNK, GDN_DK), BF16),
            pltpu.VMEM((n_chains, CHUNK, CHUNK), BF16),
            pltpu.VMEM((n_chains, 1, GDN_DV), F32),
        ],
        compiler_params=pltpu.CompilerParams(
            dimension_semantics=("parallel", "arbitrary"), vmem_limit_bytes=VMEM_LIMIT_BYTES),
        name="gdn",
    )(proj3d, proj3d, proj3d, conv_w, alog_row, dtb_row, norm_w, tril)


GLA_LEVELS = (32, 16, 8, 4, 2, 1)


def _gla_sum_matrix():
    c = CHUNK
    r = np.arange(c)[:, None]
    s = np.arange(c)[None, :]
    blocks = [(s <= r), (s > r)]
    for hs in GLA_LEVELS:
        pos = r % (2 * hs)
        m = r - pos + hs
        later = pos >= hs
        blocks.append(np.where(later, (s > m) & (s <= r), (s > r) & (s <= m)))
    return np.concatenate(blocks, axis=0).astype(np.float32)


def _gla_kernel(q_ref, k_ref, v_ref, g_ref, sm_ref, wgk_ref, bgk_ref, nw_ref, lmat_ref,
                o_ref, s_ref, *, lb):
    c = CHUNK

    @pl.when(pl.program_id(1) == 0)
    def _():
        s_ref[...] = jnp.zeros(s_ref.shape, F32)

    row = _iota2((c, c), 0)
    col = _iota2((c, c), 1)
    eye = row == col
    rowq = _iota2((c, GLA_QK_W), 0)
    lmat = lmat_ref[...]
    nw = nw_ref[...]
    scale = GLA_DK ** -0.5

    gate_pre = _dot_split(sm_ref[0], wgk_ref[...]) + bgk_ref[...]
    logg_all = (jnp.minimum(gate_pre, 0.0) - jnp.log(1.0 + jnp.exp(-jnp.abs(gate_pre)))) / GLA_NORMALIZER

    n_chunks = lb // c
    chains = [(ci, h) for ci in range(n_chunks) for h in range(GLA_HEADS)]
    rows = lambda ci: slice(ci * c, (ci + 1) * c)
    lanes = lambda h: slice(h * GLA_DK, (h + 1) * GLA_DK)

    sums = [_sum_dot(lmat, logg_all[rows(ci), :]) for ci in range(n_chunks)]
    q_in, k_out, e_last, qm, km, qk_diag = [], [], [], [], [], []
    for ci in range(n_chunks):
        q = q_ref[0, rows(ci), :] * scale
        k = k_ref[0, rows(ci), :]
        e_b = jnp.exp(sums[ci][0:c])
        e_last.append(e_b[c - 1:c, :])
        q_in.append(q * e_b)
        k_out.append(k * jnp.exp(sums[ci][c:2 * c]))
        qm_c, km_c = [], []
        for li, hs in enumerate(GLA_LEVELS):
            e_l = jnp.exp(sums[ci][(2 + li) * c:(3 + li) * c])
            later = (rowq & hs) != 0
            qm_c.append(jnp.where(later, q * e_l, 0.0))
            km_c.append(jnp.where(later, 0.0, k * e_l))
        qm.append(qm_c)
        km.append(km_c)
        qk_diag.append(q * k)

    att = {(ci, h): jnp.where(eye, jnp.sum(qk_diag[ci][:, lanes(h)], axis=-1, keepdims=True), 0.0)
           for ci, h in chains}
    for li, hs in enumerate(GLA_LEVELS):
        part = {(ci, h): _dot_nt(qm[ci][li][:, lanes(h)], km[ci][li][:, lanes(h)]) for ci, h in chains}
        for ch in chains:
            p = part[ch] if 2 * hs >= c else jnp.where((row ^ col) < 2 * hs, part[ch], 0.0)
            att[ch] = att[ch] + p
    vs = {(ci, h): v_ref[0, rows(ci), h * GLA_DV:(h + 1) * GLA_DV] for ci, h in chains}
    intra = {ch: _dot(att[ch], vs[ch]) for ch in chains}
    upd = {(ci, h): _dot_tn(vs[ci, h], k_out[ci][:, lanes(h)]) for ci, h in chains}

    state = [s_ref[h] for h in range(GLA_HEADS)]
    for ci in range(n_chunks):
        inter = [_dot_nt(q_in[ci][:, lanes(h)], state[h]) for h in range(GLA_HEADS)]
        for h in range(GLA_HEADS):
            state[h] = state[h] * e_last[ci][:, lanes(h)] + upd[ci, h]
            gg = g_ref[0, rows(ci), h * GLA_DV:(h + 1) * GLA_DV]
            o_ref[0, rows(ci), h * GLA_DV:(h + 1) * GLA_DV] = _gated_rmsnorm(inter[h] + intra[ci, h], gg, nw)
    for h in range(GLA_HEADS):
        s_ref[h] = state[h]


def _gla(proj3d, wgk_pad, bgk_row, norm_w, lmat, *, lb):
    b, l, _ = proj3d.shape
    const2 = lambda bi, ti: (0, 0)
    return pl.pallas_call(
        functools.partial(_gla_kernel, lb=lb),
        grid=(b, l // lb),
        in_specs=[
            pl.BlockSpec((1, lb, GLA_QK_W), lambda bi, ti: (bi, ti, OFF_GQ // GLA_QK_W)),
            pl.BlockSpec((1, lb, GLA_QK_W), lambda bi, ti: (bi, ti, OFF_GK // GLA_QK_W)),
            pl.BlockSpec((1, lb, GLA_V_W), lambda bi, ti: (bi, ti, OFF_GV // GLA_V_W)),
            pl.BlockSpec((1, lb, GLA_V_W), lambda bi, ti: (bi, ti, OFF_GG // GLA_V_W)),
            pl.BlockSpec((1, lb, SMALL_W), lambda bi, ti: (bi, ti, OFF_SMALL // SMALL_W)),
            pl.BlockSpec((SMALL_W, GLA_QK_W), const2),
            pl.BlockSpec((1, GLA_QK_W), const2),
            pl.BlockSpec((1, GLA_DV), const2),
            pl.BlockSpec(lmat.shape, const2),
        ],
        out_specs=pl.BlockSpec((1, lb, GLA_V_W), lambda bi, ti: (bi, ti, 0)),
        out_shape=jax.ShapeDtypeStruct((b, l, GLA_V_W), F32),
        scratch_shapes=[pltpu.VMEM((GLA_HEADS, GLA_DV, GLA_DK), F32)],
        compiler_params=pltpu.CompilerParams(
            dimension_semantics=("parallel", "arbitrary"), vmem_limit_bytes=VMEM_LIMIT_BYTES),
        name="gla",
    )(proj3d, proj3d, proj3d, proj3d, proj3d, wgk_pad, bgk_row, norm_w, lmat)


def _regroup_w_in(w_in):
    d = w_in.shape[0]
    o = 0
    qkv = w_in[:, o:o + CONV_CH]; o += CONV_CH
    z = w_in[:, o:o + GDN_V_W]; o += GDN_V_W
    beta = w_in[:, o:o + GDN_HEADS]; o += GDN_HEADS
    dec = w_in[:, o:o + GDN_HEADS]; o += GDN_HEADS
    gq = w_in[:, o:o + GLA_QK_W]; o += GLA_QK_W
    gk = w_in[:, o:o + GLA_QK_W]; o += GLA_QK_W
    gv = w_in[:, o:o + GLA_V_W]; o += GLA_V_W
    gg = w_in[:, o:o + GLA_V_W]; o += GLA_V_W
    lr = w_in[:, o:o + GLA_RANK]
    small = jnp.concatenate(
        [beta, dec, lr, jnp.zeros((d, SMALL_W - 2 * GDN_HEADS - GLA_RANK), w_in.dtype)], axis=1)
    return jnp.concatenate([qkv, z, gq, gk, gv, gg, small], axis=1)


def _lane_row(vals, offset, width):
    row = jnp.zeros((1, width), F32)
    return row.at[0, offset:offset + vals.shape[0]].set(vals.astype(F32))


def _layer(x, p, i, *, tm, lb_gdn, lb_gla):
    bsz, l, d = x.shape
    t = bsz * l
    row = lambda a: a.reshape(1, -1).astype(F32)

    h = _ffn_ln(x.reshape(t, d), p["ffn1_w_gate"][i].astype(BF16), p["ffn1_w_up"][i].astype(BF16),
                p["ffn1_w_down"][i].astype(BF16), row(p["ln1_g"][i]), row(p["ln1_b"][i]), tm=tm)

    proj = _proj(h, _regroup_w_in(p["w_in"][i]).astype(BF16), tm=tm).reshape(bsz, l, PROJ_W)

    tril = jnp.asarray(np.tile(np.tril(np.ones((CHUNK, CHUNK), np.float32)), (1, 3)), BF16)
    o_a = _gdn(proj, p["conv_w"][i].astype(F32),
               _lane_row(p["a_log"][i], GDN_HEADS, SMALL_W), _lane_row(p["dt_bias"][i], GDN_HEADS, SMALL_W),
               row(p["gdn_norm_w"][i]), tril, lb=lb_gdn)

    wgk_pad = jnp.zeros((SMALL_W, GLA_QK_W), F32).at[2 * GDN_HEADS:2 * GDN_HEADS + GLA_RANK, :].set(
        p["w_gk"][i].astype(F32))
    o_b = _gla(proj, wgk_pad, row(p["b_gk"][i]), row(p["gla_norm_w"][i]),
               jnp.asarray(np.tile(_gla_sum_matrix(), (1, 3)), BF16), lb=lb_gla)

    w_out = p["w_out"][i].astype(BF16)
    out = _mix_ffn_ln(h, o_a.reshape(t, GDN_V_W), o_b.reshape(t, GLA_V_W), w_out[:GDN_V_W], w_out[GDN_V_W:],
                      row(p["ln2_g"][i]), row(p["ln2_b"][i]),
                      p["ffn2_w_gate"][i].astype(BF16), p["ffn2_w_up"][i].astype(BF16),
                      p["ffn2_w_down"][i].astype(BF16), row(p["ln3_g"][i]), row(p["ln3_b"][i]), tm=tm)
    return out.reshape(bsz, l, d)


def kernel(x, ffn1_w_gate, ffn1_w_up, ffn1_w_down, ln1_g, ln1_b, w_in, conv_w, a_log, dt_bias, gdn_norm_w,
           w_gk, b_gk, gla_norm_w, w_out, ln2_g, ln2_b, ffn2_w_gate, ffn2_w_up, ffn2_w_down, ln3_g, ln3_b):
    p = dict(ffn1_w_gate=ffn1_w_gate, ffn1_w_up=ffn1_w_up, ffn1_w_down=ffn1_w_down, ln1_g=ln1_g, ln1_b=ln1_b,
             w_in=w_in, conv_w=conv_w, a_log=a_log, dt_bias=dt_bias, gdn_norm_w=gdn_norm_w,
             w_gk=w_gk, b_gk=b_gk, gla_norm_w=gla_norm_w, w_out=w_out, ln2_g=ln2_g, ln2_b=ln2_b,
             ffn2_w_gate=ffn2_w_gate, ffn2_w_up=ffn2_w_up, ffn2_w_down=ffn2_w_down, ln3_g=ln3_g, ln3_b=ln3_b)
    bsz, l, _ = x.shape
    tm = min(1024, bsz * l)
    for i in range(ffn1_w_gate.shape[0]):
        x = _layer(x, p, i, tm=tm, lb_gdn=min(256, l), lb_gla=min(512, l))
    return x
```

```python
import functools

import numpy as np
import jax
import jax.numpy as jnp
from jax import lax
from jax.experimental import pallas as pl
from jax.experimental.pallas import tpu as pltpu

D_MODEL = 1024
GDN_HEADS = 4
GDN_DK = 128
GDN_DV = 128
CONV_K = 4
GLA_HEADS = 4
GLA_DK = 64
GLA_DV = 128
GLA_RANK = 16
GLA_NORMALIZER = 16.0
CHUNK = 64
D_FF = 2816
DEPTH = 1
ALPHA = (2.0 * DEPTH) ** 0.25
LN_EPS = 1e-5
RMS_EPS = 1e-6

GDN_QK_W = GDN_HEADS * GDN_DK
GDN_V_W = GDN_HEADS * GDN_DV
GLA_QK_W = GLA_HEADS * GLA_DK
GLA_V_W = GLA_HEADS * GLA_DV
CONV_CH = 2 * GDN_QK_W + GDN_V_W

LANES = 128
MXU_DIM = 256
VMEM_LIMIT_BYTES = 56 * 1024 * 1024

SMALL_W = LANES
PROJ_W = CONV_CH + GDN_V_W + 2 * GLA_QK_W + 2 * GLA_V_W + SMALL_W
OFF_QKV = 0
OFF_Z = CONV_CH
OFF_GQ = OFF_Z + GDN_V_W
OFF_GK = OFF_GQ + GLA_QK_W
OFF_GV = OFF_GK + GLA_QK_W
OFF_GG = OFF_GV + GLA_V_W
OFF_SMALL = OFF_GG + GLA_V_W

BF16 = jnp.bfloat16
F32 = jnp.float32


def _dot(a, b):
    return jnp.dot(a.astype(BF16), b.astype(BF16), preferred_element_type=F32)


def _dot_nt(a, b):
    return lax.dot_general(a.astype(BF16), b.astype(BF16), (((1,), (1,)), ((), ())),
                           preferred_element_type=F32)


def _dot_tn(a, b):
    return lax.dot_general(a.astype(BF16), b.astype(BF16), (((0,), (0,)), ((), ())),
                           preferred_element_type=F32)


def _split_bf16(x, pieces):
    out = []
    for _ in range(pieces - 1):
        p = x.astype(BF16)
        out.append(p)
        x = x - p.astype(F32)
    out.append(x.astype(BF16))
    return out


def _sum_dot(sel3, x):
    return jnp.dot(sel3, jnp.concatenate(_split_bf16(x, 3), axis=0), preferred_element_type=F32)


def _dot_split(a, b):
    a_hi, a_lo = _split_bf16(a, 2)
    b_hi, b_lo = _split_bf16(b, 2)
    return jnp.dot(jnp.concatenate([a_hi, a_lo, a_hi], axis=1), jnp.concatenate([b_hi, b_hi, b_lo], axis=0),
                   preferred_element_type=F32)


NEG_LOG2_E = -1.4426950408889634


def _sigmoid(x):
    return 1.0 / (1.0 + jnp.exp2(x * NEG_LOG2_E))


def _silu(x):
    return x * _sigmoid(x)


def _softplus(x):
    return jnp.maximum(x, 0.0) + jnp.log(1.0 + jnp.exp(-jnp.abs(x)))


def _layer_norm_rows(r, g, b):
    mu = jnp.mean(r, axis=-1, keepdims=True)
    c = r - mu
    var = jnp.mean(c * c, axis=-1, keepdims=True)
    return c * lax.rsqrt(var + LN_EPS) * g + b


def _swiglu_ln(x, wg_ref, wu_ref, wd_ref, g_ref, b_ref, act_ref, ff_chunk):
    xb = x.astype(BF16)
    d_ff = wg_ref.shape[1]
    for j in range(d_ff // ff_chunk):
        sl = slice(j * ff_chunk, (j + 1) * ff_chunk)
        gate = jnp.dot(xb, wg_ref[:, sl], preferred_element_type=F32)
        up = jnp.dot(xb, wu_ref[:, sl], preferred_element_type=F32)
        act_ref[:, sl] = (_silu(gate) * up).astype(BF16)
    y = jnp.dot(act_ref[...], wd_ref[...], preferred_element_type=F32)
    return _layer_norm_rows(ALPHA * x + 0.5 * y, g_ref[...], b_ref[...])


def _ffn_ln_kernel(x_ref, wg_ref, wu_ref, wd_ref, g_ref, b_ref, o_ref, act_ref, *, ff_chunk):
    o_ref[...] = _swiglu_ln(x_ref[...], wg_ref, wu_ref, wd_ref, g_ref, b_ref, act_ref, ff_chunk)


def _mix_ffn_ln_kernel(h_ref, oa_ref, ob_ref, wa_ref, wb_ref, g2_ref, b2_ref,
                       wg_ref, wu_ref, wd_ref, g3_ref, b3_ref, o_ref, act_ref, *, ff_chunk):
    mix = (jnp.dot(oa_ref[...].astype(BF16), wa_ref[...], preferred_element_type=F32)
           + jnp.dot(ob_ref[...].astype(BF16), wb_ref[...], preferred_element_type=F32))
    x = _layer_norm_rows(ALPHA * h_ref[...] + mix, g2_ref[...], b2_ref[...])
    o_ref[...] = _swiglu_ln(x, wg_ref, wu_ref, wd_ref, g3_ref, b3_ref, act_ref, ff_chunk)


def _mix_ffn_ln(h2d, oa2d, ob2d, w_a, w_b, ln2_g, ln2_b, wg, wu, wd, ln3_g, ln3_b, *, tm):
    t, d = h2d.shape
    d_ff = wg.shape[1]
    const = lambda i: (0, 0)
    tile = lambda w: pl.BlockSpec((tm, w), lambda i: (i, 0))
    resident = lambda a: pl.BlockSpec(a.shape, const, pipeline_mode=pl.Buffered(1))
    return pl.pallas_call(
        functools.partial(_mix_ffn_ln_kernel, ff_chunk=MXU_DIM),
        grid=(t // tm,),
        in_specs=[tile(d), tile(oa2d.shape[1]), tile(ob2d.shape[1]), resident(w_a), resident(w_b),
                  pl.BlockSpec((1, d), const), pl.BlockSpec((1, d), const),
                  resident(wg), resident(wu), resident(wd),
                  pl.BlockSpec((1, d), const), pl.BlockSpec((1, d), const)],
        out_specs=tile(d),
        out_shape=jax.ShapeDtypeStruct((t, d), F32),
        scratch_shapes=[pltpu.VMEM((tm, d_ff), BF16)],
        compiler_params=pltpu.CompilerParams(
            dimension_semantics=("parallel",), vmem_limit_bytes=VMEM_LIMIT_BYTES),
        name="mix_ffn_ln",
    )(h2d, oa2d, ob2d, w_a, w_b, ln2_g, ln2_b, wg, wu, wd, ln3_g, ln3_b)


def _ffn_ln(x2d, wg, wu, wd, ln_g, ln_b, *, tm):
    t, d = x2d.shape
    d_ff = wg.shape[1]
    const = lambda i: (0, 0)
    return pl.pallas_call(
        functools.partial(_ffn_ln_kernel, ff_chunk=MXU_DIM),
        grid=(t // tm,),
        in_specs=[
            pl.BlockSpec((tm, d), lambda i: (i, 0)),
            pl.BlockSpec((d, d_ff), const, pipeline_mode=pl.Buffered(1)),
            pl.BlockSpec((d, d_ff), const, pipeline_mode=pl.Buffered(1)),
            pl.BlockSpec((d_ff, d), const, pipeline_mode=pl.Buffered(1)),
            pl.BlockSpec((1, d), const),
            pl.BlockSpec((1, d), const),
        ],
        out_specs=pl.BlockSpec((tm, d), lambda i: (i, 0)),
        out_shape=jax.ShapeDtypeStruct((t, d), F32),
        scratch_shapes=[pltpu.VMEM((tm, d_ff), BF16)],
        compiler_params=pltpu.CompilerParams(
            dimension_semantics=("parallel",), vmem_limit_bytes=VMEM_LIMIT_BYTES),
        name="ffn_ln",
    )(x2d, wg, wu, wd, ln_g, ln_b)


def _proj_kernel(h_ref, w_ref, o_ref):
    o_ref[...] = jnp.dot(h_ref[...].astype(BF16), w_ref[...], preferred_element_type=F32)


def _proj(h2d, w, *, tm):
    t, d = h2d.shape
    n = w.shape[1]
    return pl.pallas_call(
        _proj_kernel,
        grid=(t // tm,),
        in_specs=[
            pl.BlockSpec((tm, d), lambda i: (i, 0)),
            pl.BlockSpec((d, n), lambda i: (0, 0), pipeline_mode=pl.Buffered(1)),
        ],
        out_specs=pl.BlockSpec((tm, n), lambda i: (i, 0)),
        out_shape=jax.ShapeDtypeStruct((t, n), F32),
        compiler_params=pltpu.CompilerParams(
            dimension_semantics=("parallel",), vmem_limit_bytes=VMEM_LIMIT_BYTES),
        name="proj",
    )(h2d, w)


def _iota2(shape, dim):
    return lax.broadcasted_iota(jnp.int32, shape, dim)


def _gated_rmsnorm(o, gate, w):
    ms = jnp.mean(o * o, axis=-1, keepdims=True)
    return o * lax.rsqrt(ms + RMS_EPS) * w * _silu(gate)


def _gdn_kernel(qkv_ref, z_ref, sm_ref, convw_ref, alog_ref, dtb_ref, nw_ref, tril_ref,
                o_ref, xpad_ref, act_ref, s_ref, wq_ref, u_ref, kd_ref, qk_ref, dl_ref, *, lb):
    c = CHUNK
    halo = 8
    n_chunks = lb // c
    chains = [(ci, h) for ci in range(n_chunks) for h in range(GDN_HEADS)]
    rows = lambda ci: slice(ci * c, (ci + 1) * c)
    nw = nw_ref[...]

    @pl.when(pl.program_id(1) == 0)
    def _():
        xpad_ref[:, 0:halo, :] = jnp.zeros((CONV_CH // LANES, halo, LANES), F32)
        for ref in (s_ref, wq_ref, u_ref, kd_ref, qk_ref, dl_ref):
            ref[...] = jnp.zeros(ref.shape, ref.dtype)

    state = [s_ref[h] for h in range(GDN_HEADS)]
    for ci in range(n_chunks):
        heads = [(ci, h) for h in range(GDN_HEADS)]
        ws = {(ci, h): _dot(wq_ref[ci * GDN_HEADS + h], state[h]) for ci, h in heads}
        v_new = {(ci, h): u_ref[ci * GDN_HEADS + h] - ws[ci, h][:c] for ci, h in heads}
        upd = {(ci, h): _dot_tn(kd_ref[ci * GDN_HEADS + h], v_new[ci, h]) for ci, h in heads}
        intra = {(ci, h): _dot(qk_ref[ci * GDN_HEADS + h], v_new[ci, h]) for ci, h in heads}
        for ch in heads:
            h = ch[1]
            state[h] = state[h] * dl_ref[ci * GDN_HEADS + h] + upd[ch]
            zg = z_ref[0, rows(ci), h * GDN_DV:(h + 1) * GDN_DV]
            o_ref[0, rows(ci), h * GDN_DV:(h + 1) * GDN_DV] = _gated_rmsnorm(ws[ch][c:] + intra[ch], zg, nw)
    for h in range(GDN_HEADS):
        s_ref[h] = state[h]

    pitch = (halo + lb) // 8
    assert pitch * 8 == halo + lb and pitch % 8 != 0
    for j in range(CONV_CH // LANES):
        ls = slice(j * LANES, (j + 1) * LANES)
        xpad_ref[j, halo:halo + lb, :] = qkv_ref[0, :, ls]
        taps = [jnp.broadcast_to(convw_ref[k:k + 1, ls], (8, LANES)) for k in range(CONV_K)]
        tiles = [xpad_ref[j, pl.ds(g, 8, stride=pitch), :] for g in range(pitch)]
        wrapped = {-k: pltpu.roll(tiles[pitch - k], 1, 0) for k in range(1, CONV_K)}
        back = lambda g: tiles[g] if g >= 0 else wrapped[g]
        is_qk = j < 2 * GDN_HEADS
        scale = GDN_DK ** -0.5 if j < GDN_HEADS else 1.0
        for g in range(pitch):
            y = taps[CONV_K - 1] * tiles[g]
            for k in range(CONV_K - 1):
                y = y + taps[k] * back(g - (CONV_K - 1 - k))
            y = _silu(y)
            if is_qk:
                y = y * (lax.rsqrt(jnp.sum(y * y, axis=-1, keepdims=True) + RMS_EPS) * scale)
            act_ref[j, pl.ds(g, 8, stride=pitch), :] = y
        xpad_ref[j, 0:halo, :] = xpad_ref[j, lb:lb + halo, :]

    sm = sm_ref[0]
    beta_all = _sigmoid(sm)
    g_all = -jnp.exp(alog_ref[...]) * _softplus(sm + dtb_ref[...])

    row = _iota2((c, c), 0)
    col = _iota2((c, c), 1)
    causal = row >= col
    strict = row > col
    eye = (row == col).astype(F32)
    tril = tril_ref[...]


    g_cum = [_sum_dot(tril, g_all[rows(ci), :]) for ci in range(n_chunks)]
    g_cum_t = [jnp.transpose(jnp.concatenate([g, g], axis=0))[:, :c] for g in g_cum]

    qs, ks, vs, betas, gcs, decays, e_gs, g_lasts = {}, {}, {}, {}, {}, {}, {}, {}
    for ci, h in chains:
        rs = rows(ci)
        ars = slice(halo + ci * c, halo + (ci + 1) * c)
        qs[ci, h] = act_ref[h, ars, :]
        ks[ci, h] = act_ref[GDN_HEADS + h, ars, :]
        vs[ci, h] = act_ref[2 * GDN_HEADS + h, ars, :]
        betas[ci, h] = beta_all[rs, h:h + 1]
        gc = g_cum[ci][:, GDN_HEADS + h:GDN_HEADS + h + 1]
        gr = g_cum_t[ci][GDN_HEADS + h:GDN_HEADS + h + 1, :]
        gcs[ci, h] = gc
        g_lasts[ci, h] = gc[c - 1:c, :]
        decays[ci, h] = jnp.where(causal, jnp.exp(jnp.where(causal, gc - gr, 0.0)), 0.0)
        e_gs[ci, h] = jnp.exp(gc)

    qk_kk = {ch: _dot_nt(jnp.concatenate([qs[ch], ks[ch]], axis=0), ks[ch]) for ch in chains}
    qk = {ch: qk_kk[ch][:c] * decays[ch] for ch in chains}
    a_neg = {ch: jnp.where(strict, -(betas[ch] * qk_kk[ch][c:] * decays[ch]), 0.0) for ch in chains}
    a_bf = {ch: a_neg[ch].astype(BF16) for ch in chains}
    n_pow = {ch: _dot(a_bf[ch], a_bf[ch]).astype(BF16) for ch in chains}
    t_inv = {ch: eye + a_neg[ch] for ch in chains}
    for _ in range(4):
        y = {ch: _dot(jnp.concatenate([n_pow[ch], t_inv[ch].astype(BF16)], axis=0), n_pow[ch]) for ch in chains}
        n_pow = {ch: y[ch][:c].astype(BF16) for ch in chains}
        t_inv = {ch: t_inv[ch] + y[ch][c:] for ch in chains}
    corr = {ch: _dot(t_inv[ch], n_pow[ch]) for ch in chains}
    t_inv = {ch: t_inv[ch] + corr[ch] for ch in chains}

    sol = {ch: _dot(t_inv[ch], jnp.concatenate(
        [vs[ch] * betas[ch], ks[ch] * (betas[ch] * e_gs[ch])], axis=1)) for ch in chains}
    wq = {ch: jnp.concatenate([sol[ch][:, GDN_DV:], qs[ch] * e_gs[ch]], axis=0) for ch in chains}
    k_dec = {ch: ks[ch] * jnp.exp(g_lasts[ch] - gcs[ch]) for ch in chains}

    for ci, h in chains:
        i = ci * GDN_HEADS + h
        wq_ref[i] = wq[ci, h].astype(BF16)
        u_ref[i] = sol[ci, h][:, :GDN_DV]
        kd_ref[i] = k_dec[ci, h].astype(BF16)
        qk_ref[i] = qk[ci, h].astype(BF16)
        dl_ref[i] = jnp.broadcast_to(jnp.exp(g_lasts[ci, h]), (1, GDN_DV))


def _gdn(proj3d, conv_w, alog_row, dtb_row, norm_w, tril, *, lb):
    b, l, _ = proj3d.shape
    nb = l // lb
    n_chains = (lb // CHUNK) * GDN_HEADS
    const2 = lambda bi, ti: (0, 0)
    prep = lambda ti: jnp.minimum(ti, nb - 1)
    scan = lambda ti: jnp.maximum(ti - 1, 0)
    return pl.pallas_call(
        functools.partial(_gdn_kernel, lb=lb),
        grid=(b, nb + 1),
        in_specs=[
            pl.BlockSpec((1, lb, CONV_CH), lambda bi, ti: (bi, prep(ti), OFF_QKV // CONV_CH)),
            pl.BlockSpec((1, lb, GDN_V_W), lambda bi, ti: (bi, scan(ti), OFF_Z // GDN_V_W)),
            pl.BlockSpec((1, lb, SMALL_W), lambda bi, ti: (bi, prep(ti), OFF_SMALL // SMALL_W)),
            pl.BlockSpec((CONV_K, CONV_CH), const2),
            pl.BlockSpec((1, SMALL_W), const2),
            pl.BlockSpec((1, SMALL_W), const2),
            pl.BlockSpec((1, GDN_DV), const2),
            pl.BlockSpec((CHUNK, 3 * CHUNK), const2),
        ],
        out_specs=pl.BlockSpec((1, lb, GDN_V_W), lambda bi, ti: (bi, scan(ti), 0)),
        out_shape=jax.ShapeDtypeStruct((b, l, GDN_V_W), F32),
        scratch_shapes=[
            pltpu.VMEM((CONV_CH // LANES, lb + 8, LANES), F32),
            pltpu.VMEM((CONV_CH // LANES, lb + 8, LANES), F32),
            pltpu.VMEM((GDN_HEADS, GDN_DK, GDN_DV), F32),
            pltpu.VMEM((n_chains, 2 * CHUNK, GDN_DK), BF16),
            pltpu.VMEM((n_chains, CHUNK, GDN_DV), F32),
            pltpu.VMEM((n_chains, CHUNK, GDN_DK), BF16),
            pltpu.VMEM((n_chains, CHUNK, CHUNK), BF16),
            pltpu.VMEM((n_chains, 1, GDN_DV), F32),
        ],
        compiler_params=pltpu.CompilerParams(
            dimension_semantics=("parallel", "arbitrary"), vmem_limit_bytes=VMEM_LIMIT_BYTES),
        name="gdn",
    )(proj3d, proj3d, proj3d, conv_w, alog_row, dtb_row, norm_w, tril)


GLA_LEVELS = (32, 16, 8, 4, 2, 1)


def _gla_sum_matrix():
    c = CHUNK
    r = np.arange(c)[:, None]
    s = np.arange(c)[None, :]
    blocks = [(s <= r), (s > r)]
    for hs in GLA_LEVELS:
        pos = r % (2 * hs)
        m = r - pos + hs
        later = pos >= hs
        blocks.append(np.where(later, (s > m) & (s <= r), (s > r) & (s <= m)))
    return np.concatenate(blocks, axis=0).astype(np.float32)


def _gla_kernel(q_ref, k_ref, v_ref, g_ref, sm_ref, wgk_ref, bgk_ref, nw_ref, lmat_ref,
                o_ref, s_ref, *, lb):
    c = CHUNK

    @pl.when(pl.program_id(1) == 0)
    def _():
        s_ref[...] = jnp.zeros(s_ref.shape, F32)

    row = _iota2((c, c), 0)
    col = _iota2((c, c), 1)
    eye = row == col
    lmat = lmat_ref[...]
    nw = nw_ref[...]
    scale = GLA_DK ** -0.5

    gate_pre = _dot_split(sm_ref[0], wgk_ref[...]) + bgk_ref[...]
    logg_all = (jnp.minimum(gate_pre, 0.0) - jnp.log(1.0 + jnp.exp(-jnp.abs(gate_pre)))) / GLA_NORMALIZER

    n_chunks = lb // c
    chains = [(ci, h) for ci in range(n_chunks) for h in range(GLA_HEADS)]
    rows = lambda ci: slice(ci * c, (ci + 1) * c)
    lanes = lambda h: slice(h * GLA_DK, (h + 1) * GLA_DK)

    sums = [_sum_dot(lmat, logg_all[rows(ci), :]) for ci in range(n_chunks)]
    q_in, k_out, e_last, qm, km, qk_diag = [], [], [], [], [], []
    for ci in range(n_chunks):
        q = q_ref[0, rows(ci), :] * scale
        k = k_ref[0, rows(ci), :]
        e_b = jnp.exp(sums[ci][0:c])
        e_last.append(e_b[c - 1:c, :])
        q_in.append(q * e_b)
        k_out.append(k * jnp.exp(sums[ci][c:2 * c]))
        qm_c, km_c = [], []
        for li, hs in enumerate(GLA_LEVELS):
            e_l = jnp.exp(sums[ci][(2 + li) * c:(3 + li) * c])
            qm_c.append(q * e_l)
            km_c.append(k * e_l)
        qm.append(qm_c)
        km.append(km_c)
        qk_diag.append(q * k)

    att = {(ci, h): jnp.where(eye, jnp.sum(qk_diag[ci][:, lanes(h)], axis=-1, keepdims=True), 0.0)
           for ci, h in chains}
    for li, hs in enumerate(GLA_LEVELS):
        part = {(ci, h): _dot_nt(qm[ci][li][:, lanes(h)], km[ci][li][:, lanes(h)]) for ci, h in chains}
        keep = ((row ^ col) < 2 * hs) & ((row & hs) != 0) & ((col & hs) == 0)
        for ch in chains:
            att[ch] = att[ch] + jnp.where(keep, part[ch], 0.0)
    vs = {(ci, h): v_ref[0, rows(ci), h * GLA_DV:(h + 1) * GLA_DV] for ci, h in chains}
    intra = {ch: _dot(att[ch], vs[ch]) for ch in chains}
    upd = {(ci, h): _dot_tn(vs[ci, h], k_out[ci][:, lanes(h)]) for ci, h in chains}

    state = [s_ref[h] for h in range(GLA_HEADS)]
    for ci in range(n_chunks):
        inter = [_dot_nt(q_in[ci][:, lanes(h)], state[h]) for h in range(GLA_HEADS)]
        for h in range(GLA_HEADS):
            state[h] = state[h] * e_last[ci][:, lanes(h)] + upd[ci, h]
            gg = g_ref[0, rows(ci), h * GLA_DV:(h + 1) * GLA_DV]
            o_ref[0, rows(ci), h * GLA_DV:(h + 1) * GLA_DV] = _gated_rmsnorm(inter[h] + intra[ci, h], gg, nw)
    for h in range(GLA_HEADS):
        s_ref[h] = state[h]


def _gla(proj3d, wgk_pad, bgk_row, norm_w, lmat, *, lb):
    b, l, _ = proj3d.shape
    const2 = lambda bi, ti: (0, 0)
    return pl.pallas_call(
        functools.partial(_gla_kernel, lb=lb),
        grid=(b, l // lb),
        in_specs=[
            pl.BlockSpec((1, lb, GLA_QK_W), lambda bi, ti: (bi, ti, OFF_GQ // GLA_QK_W)),
            pl.BlockSpec((1, lb, GLA_QK_W), lambda bi, ti: (bi, ti, OFF_GK // GLA_QK_W)),
            pl.BlockSpec((1, lb, GLA_V_W), lambda bi, ti: (bi, ti, OFF_GV // GLA_V_W)),
            pl.BlockSpec((1, lb, GLA_V_W), lambda bi, ti: (bi, ti, OFF_GG // GLA_V_W)),
            pl.BlockSpec((1, lb, SMALL_W), lambda bi, ti: (bi, ti, OFF_SMALL // SMALL_W)),
            pl.BlockSpec((SMALL_W, GLA_QK_W), const2),
            pl.BlockSpec((1, GLA_QK_W), const2),
            pl.BlockSpec((1, GLA_DV), const2),
            pl.BlockSpec(lmat.shape, const2),
        ],
        out_specs=pl.BlockSpec((1, lb, GLA_V_W), lambda bi, ti: (bi, ti, 0)),
        out_shape=jax.ShapeDtypeStruct((b, l, GLA_V_W), F32),
        scratch_shapes=[pltpu.VMEM((GLA_HEADS, GLA_DV, GLA_DK), F32)],
        compiler_params=pltpu.CompilerParams(
            dimension_semantics=("parallel", "arbitrary"), vmem_limit_bytes=VMEM_LIMIT_BYTES),
        name="gla",
    )(proj3d, proj3d, proj3d, proj3d, proj3d, wgk_pad, bgk_row, norm_w, lmat)


def _regroup_w_in(w_in):
    d = w_in.shape[0]
    o = 0
    qkv = w_in[:, o:o + CONV_CH]; o += CONV_CH
    z = w_in[:, o:o + GDN_V_W]; o += GDN_V_W
    beta = w_in[:, o:o + GDN_HEADS]; o += GDN_HEADS
    dec = w_in[:, o:o + GDN_HEADS]; o += GDN_HEADS
    gq = w_in[:, o:o + GLA_QK_W]; o += GLA_QK_W
    gk = w_in[:, o:o + GLA_QK_W]; o += GLA_QK_W
    gv = w_in[:, o:o + GLA_V_W]; o += GLA_V_W
    gg = w_in[:, o:o + GLA_V_W]; o += GLA_V_W
    lr = w_in[:, o:o + GLA_RANK]
    small = jnp.concatenate(
        [beta, dec, lr, jnp.zeros((d, SMALL_W - 2 * GDN_HEADS - GLA_RANK), w_in.dtype)], axis=1)
    return jnp.concatenate([qkv, z, gq, gk, gv, gg, small], axis=1)


def _lane_row(vals, offset, width):
    row = jnp.zeros((1, width), F32)
    return row.at[0, offset:offset + vals.shape[0]].set(vals.astype(F32))


def _layer(x, p, i, *, tm, lb_gdn, lb_gla):
    bsz, l, d = x.shape
    t = bsz * l
    row = lambda a: a.reshape(1, -1).astype(F32)

    h = _ffn_ln(x.reshape(t, d), p["ffn1_w_gate"][i].astype(BF16), p["ffn1_w_up"][i].astype(BF16),
                p["ffn1_w_down"][i].astype(BF16), row(p["ln1_g"][i]), row(p["ln1_b"][i]), tm=tm)

    proj = _proj(h, _regroup_w_in(p["w_in"][i].astype(BF16)), tm=tm).reshape(bsz, l, PROJ_W)

    tril = jnp.asarray(np.tile(np.tril(np.ones((CHUNK, CHUNK), np.float32)), (1, 3)), BF16)
    o_a = _gdn(proj, p["conv_w"][i].astype(F32),
               _lane_row(p["a_log"][i], GDN_HEADS, SMALL_W), _lane_row(p["dt_bias"][i], GDN_HEADS, SMALL_W),
               row(p["gdn_norm_w"][i]), tril, lb=lb_gdn)

    wgk_pad = jnp.zeros((SMALL_W, GLA_QK_W), F32).at[2 * GDN_HEADS:2 * GDN_HEADS + GLA_RANK, :].set(
        p["w_gk"][i].astype(F32))
    o_b = _gla(proj, wgk_pad, row(p["b_gk"][i]), row(p["gla_norm_w"][i]),
               jnp.asarray(np.tile(_gla_sum_matrix(), (1, 3)), BF16), lb=lb_gla)

    w_out = p["w_out"][i].astype(BF16)
    out = _mix_ffn_ln(h, o_a.reshape(t, GDN_V_W), o_b.reshape(t, GLA_V_W), w_out[:GDN_V_W], w_out[GDN_V_W:],
                      row(p["ln2_g"][i]), row(p["ln2_b"][i]),
                      p["ffn2_w_gate"][i].astype(BF16), p["ffn2_w_up"][i].astype(BF16),
                      p["ffn2_w_down"][i].astype(BF16), row(p["ln3_g"][i]), row(p["ln3_b"][i]), tm=tm)
    return out.reshape(bsz, l, d)


def kernel(x, ffn1_w_gate, ffn1_w_up, ffn1_w_down, ln1_g, ln1_b, w_in, conv_w, a_log, dt_bias, gdn_norm_w,
           w_gk, b_gk, gla_norm_w, w_out, ln2_g, ln2_b, ffn2_w_gate, ffn2_w_up, ffn2_w_down, ln3_g, ln3_b):
    p = dict(ffn1_w_gate=ffn1_w_gate, ffn1_w_up=ffn1_w_up, ffn1_w_down=ffn1_w_down, ln1_g=ln1_g, ln1_b=ln1_b,
             w_in=w_in, conv_w=conv_w, a_log=a_log, dt_bias=dt_bias, gdn_norm_w=gdn_norm_w,
             w_gk=w_gk, b_gk=b_gk, gla_norm_w=gla_norm_w, w_out=w_out, ln2_g=ln2_g, ln2_b=ln2_b,
             ffn2_w_gate=ffn2_w_gate, ffn2_w_up=ffn2_w_up, ffn2_w_down=ffn2_w_down, ln3_g=ln3_g, ln3_b=ln3_b)
    bsz, l, _ = x.shape
    tm = min(1024, bsz * l)
    for i in range(ffn1_w_gate.shape[0]):
        x = _layer(x, p, i, tm=tm, lb_gdn=min(256, l), lb_gla=min(512, l))
    return x
```

```python
import functools

import numpy as np
import jax
import jax.numpy as jnp
from jax import lax
from jax.experimental import pallas as pl
from jax.experimental.pallas import tpu as pltpu

D_MODEL = 1024
GDN_HEADS = 4
GDN_DK = 128
GDN_DV = 128
CONV_K = 4
GLA_HEADS = 4
GLA_DK = 64
GLA_DV = 128
GLA_RANK = 16
GLA_NORMALIZER = 16.0
CHUNK = 64
D_FF = 2816
DEPTH = 1
ALPHA = (2.0 * DEPTH) ** 0.25
LN_EPS = 1e-5
RMS_EPS = 1e-6

GDN_QK_W = GDN_HEADS * GDN_DK
GDN_V_W = GDN_HEADS * GDN_DV
GLA_QK_W = GLA_HEADS * GLA_DK
GLA_V_W = GLA_HEADS * GLA_DV
CONV_CH = 2 * GDN_QK_W + GDN_V_W

LANES = 128
MXU_DIM = 256
VMEM_LIMIT_BYTES = 56 * 1024 * 1024

SMALL_W = LANES
PROJ_W = CONV_CH + GDN_V_W + 2 * GLA_QK_W + 2 * GLA_V_W + SMALL_W
OFF_QKV = 0
OFF_Z = CONV_CH
OFF_GQ = OFF_Z + GDN_V_W
OFF_GK = OFF_GQ + GLA_QK_W
OFF_GV = OFF_GK + GLA_QK_W
OFF_GG = OFF_GV + GLA_V_W
OFF_SMALL = OFF_GG + GLA_V_W

BF16 = jnp.bfloat16
F32 = jnp.float32


def _dot(a, b):
    return jnp.dot(a.astype(BF16), b.astype(BF16), preferred_element_type=F32)


def _dot_nt(a, b):
    return lax.dot_general(a.astype(BF16), b.astype(BF16), (((1,), (1,)), ((), ())),
                           preferred_element_type=F32)


def _dot_tn(a, b):
    return lax.dot_general(a.astype(BF16), b.astype(BF16), (((0,), (0,)), ((), ())),
                           preferred_element_type=F32)


def _split_bf16(x, pieces):
    out = []
    for _ in range(pieces - 1):
        p = x.astype(BF16)
        out.append(p)
        x = x - p.astype(F32)
    out.append(x.astype(BF16))
    return out


def _sum_dot(sel3, x):
    return jnp.dot(sel3, jnp.concatenate(_split_bf16(x, 3), axis=0), preferred_element_type=F32)


def _dot_split(a, b):
    a_hi, a_lo = _split_bf16(a, 2)
    b_hi, b_lo = _split_bf16(b, 2)
    return jnp.dot(jnp.concatenate([a_hi, a_lo, a_hi], axis=1), jnp.concatenate([b_hi, b_hi, b_lo], axis=0),
                   preferred_element_type=F32)


NEG_LOG2_E = -1.4426950408889634


def _sigmoid(x):
    return 1.0 / (1.0 + jnp.exp2(x * NEG_LOG2_E))


def _silu(x):
    return x * _sigmoid(x)


def _softplus(x):
    return jnp.maximum(x, 0.0) + jnp.log(1.0 + jnp.exp(-jnp.abs(x)))


def _layer_norm_rows(r, g, b):
    mu = jnp.mean(r, axis=-1, keepdims=True)
    c = r - mu
    var = jnp.mean(c * c, axis=-1, keepdims=True)
    return c * lax.rsqrt(var + LN_EPS) * g + b


ROW_SPLITS = 4


def _row_groups(tm):
    step = tm // ROW_SPLITS
    return [slice(r * step, (r + 1) * step) for r in range(ROW_SPLITS)]


def _swiglu_ln(x, wg_ref, wu_ref, wd_ref, g_ref, b_ref, act_ref, o_ref, ff_chunk):
    xb = x.astype(BF16)
    d_ff = wg_ref.shape[1]
    for j in range(d_ff // ff_chunk):
        sl = slice(j * ff_chunk, (j + 1) * ff_chunk)
        gate = jnp.dot(xb, wg_ref[:, sl], preferred_element_type=F32)
        up = jnp.dot(xb, wu_ref[:, sl], preferred_element_type=F32)
        act_ref[:, sl] = (_silu(gate) * up).astype(BF16)
    for rs in _row_groups(x.shape[0]):
        y = jnp.dot(act_ref[rs, :], wd_ref[...], preferred_element_type=F32)
        o_ref[rs, :] = _layer_norm_rows(ALPHA * x[rs, :] + 0.5 * y, g_ref[...], b_ref[...])


def _ffn_ln_kernel(x_ref, wg_ref, wu_ref, wd_ref, g_ref, b_ref, o_ref, act_ref, *, ff_chunk):
    _swiglu_ln(x_ref[...], wg_ref, wu_ref, wd_ref, g_ref, b_ref, act_ref, o_ref, ff_chunk)


def _mix_ffn_ln_kernel(h_ref, oa_ref, ob_ref, wa_ref, wb_ref, g2_ref, b2_ref,
                       wg_ref, wu_ref, wd_ref, g3_ref, b3_ref, o_ref, act_ref, *, ff_chunk):
    xs = []
    for rs in _row_groups(h_ref.shape[0]):
        mix = (jnp.dot(oa_ref[rs, :].astype(BF16), wa_ref[...], preferred_element_type=F32)
               + jnp.dot(ob_ref[rs, :].astype(BF16), wb_ref[...], preferred_element_type=F32))
        xs.append(_layer_norm_rows(ALPHA * h_ref[rs, :] + mix, g2_ref[...], b2_ref[...]))
    _swiglu_ln(jnp.concatenate(xs, axis=0), wg_ref, wu_ref, wd_ref, g3_ref, b3_ref, act_ref, o_ref, ff_chunk)


def _mix_ffn_ln(h2d, oa2d, ob2d, w_a, w_b, ln2_g, ln2_b, wg, wu, wd, ln3_g, ln3_b, *, tm):
    t, d = h2d.shape
    d_ff = wg.shape[1]
    const = lambda i: (0, 0)
    tile = lambda w: pl.BlockSpec((tm, w), lambda i: (i, 0))
    resident = lambda a: pl.BlockSpec(a.shape, const, pipeline_mode=pl.Buffered(1))
    return pl.pallas_call(
        functools.partial(_mix_ffn_ln_kernel, ff_chunk=MXU_DIM),
        grid=(t // tm,),
        in_specs=[tile(d), tile(oa2d.shape[1]), tile(ob2d.shape[1]), resident(w_a), resident(w_b),
                  pl.BlockSpec((1, d), const), pl.BlockSpec((1, d), const),
                  resident(wg), resident(wu), resident(wd),
                  pl.BlockSpec((1, d), const), pl.BlockSpec((1, d), const)],
        out_specs=tile(d),
        out_shape=jax.ShapeDtypeStruct((t, d), F32),
        scratch_shapes=[pltpu.VMEM((tm, d_ff), BF16)],
        compiler_params=pltpu.CompilerParams(
            dimension_semantics=("parallel",), vmem_limit_bytes=VMEM_LIMIT_BYTES),
        name="mix_ffn_ln",
    )(h2d, oa2d, ob2d, w_a, w_b, ln2_g, ln2_b, wg, wu, wd, ln3_g, ln3_b)


def _ffn_ln(x2d, wg, wu, wd, ln_g, ln_b, *, tm):
    t, d = x2d.shape
    d_ff = wg.shape[1]
    const = lambda i: (0, 0)
    return pl.pallas_call(
        functools.partial(_ffn_ln_kernel, ff_chunk=MXU_DIM),
        grid=(t // tm,),
        in_specs=[
            pl.BlockSpec((tm, d), lambda i: (i, 0)),
            pl.BlockSpec((d, d_ff), const, pipeline_mode=pl.Buffered(1)),
            pl.BlockSpec((d, d_ff), const, pipeline_mode=pl.Buffered(1)),
            pl.BlockSpec((d_ff, d), const, pipeline_mode=pl.Buffered(1)),
            pl.BlockSpec((1, d), const),
            pl.BlockSpec((1, d), const),
        ],
        out_specs=pl.BlockSpec((tm, d), lambda i: (i, 0)),
        out_shape=jax.ShapeDtypeStruct((t, d), F32),
        scratch_shapes=[pltpu.VMEM((tm, d_ff), BF16)],
        compiler_params=pltpu.CompilerParams(
            dimension_semantics=("parallel",), vmem_limit_bytes=VMEM_LIMIT_BYTES),
        name="ffn_ln",
    )(x2d, wg, wu, wd, ln_g, ln_b)


def _proj_kernel(h_ref, w_ref, o_ref):
    o_ref[...] = jnp.dot(h_ref[...].astype(BF16), w_ref[...], preferred_element_type=F32)


def _proj(h2d, w, *, tm):
    t, d = h2d.shape
    n = w.shape[1]
    return pl.pallas_call(
        _proj_kernel,
        grid=(t // tm,),
        in_specs=[
            pl.BlockSpec((tm, d), lambda i: (i, 0)),
            pl.BlockSpec((d, n), lambda i: (0, 0), pipeline_mode=pl.Buffered(1)),
        ],
        out_specs=pl.BlockSpec((tm, n), lambda i: (i, 0)),
        out_shape=jax.ShapeDtypeStruct((t, n), F32),
        compiler_params=pltpu.CompilerParams(
            dimension_semantics=("parallel",), vmem_limit_bytes=VMEM_LIMIT_BYTES),
        name="proj",
    )(h2d, w)


def _iota2(shape, dim):
    return lax.broadcasted_iota(jnp.int32, shape, dim)


def _gated_rmsnorm(o, gate, w):
    ms = jnp.mean(o * o, axis=-1, keepdims=True)
    return o * lax.rsqrt(ms + RMS_EPS) * w * _silu(gate)


def _gdn_kernel(qkv_ref, z_ref, sm_ref, convw_ref, alog_ref, dtb_ref, nw_ref, tril_ref,
                o_ref, xpad_ref, act_ref, s_ref, wq_ref, u_ref, kd_ref, qk_ref, dl_ref, *, lb):
    c = CHUNK
    halo = 8
    n_chunks = lb // c
    chains = [(ci, h) for ci in range(n_chunks) for h in range(GDN_HEADS)]
    rows = lambda ci: slice(ci * c, (ci + 1) * c)
    nw = nw_ref[...]

    @pl.when(pl.program_id(1) == 0)
    def _():
        xpad_ref[:, 0:halo, :] = jnp.zeros((CONV_CH // LANES, halo, LANES), F32)
        for ref in (s_ref, wq_ref, u_ref, kd_ref, qk_ref, dl_ref):
            ref[...] = jnp.zeros(ref.shape, ref.dtype)

    state = [s_ref[h] for h in range(GDN_HEADS)]
    for ci in range(n_chunks):
        heads = [(ci, h) for h in range(GDN_HEADS)]
        ws = {(ci, h): _dot(wq_ref[ci * GDN_HEADS + h], state[h]) for ci, h in heads}
        v_new = {(ci, h): u_ref[ci * GDN_HEADS + h] - ws[ci, h][:c] for ci, h in heads}
        upd = {(ci, h): _dot_tn(kd_ref[ci * GDN_HEADS + h], v_new[ci, h]) for ci, h in heads}
        intra = {(ci, h): _dot(qk_ref[ci * GDN_HEADS + h], v_new[ci, h]) for ci, h in heads}
        for ch in heads:
            h = ch[1]
            state[h] = state[h] * dl_ref[ci * GDN_HEADS + h] + upd[ch]
            zg = z_ref[0, rows(ci), h * GDN_DV:(h + 1) * GDN_DV]
            o_ref[0, rows(ci), h * GDN_DV:(h + 1) * GDN_DV] = _gated_rmsnorm(ws[ch][c:] + intra[ch], zg, nw)
    for h in range(GDN_HEADS):
        s_ref[h] = state[h]

    pitch = (halo + lb) // 8
    assert pitch * 8 == halo + lb and pitch % 8 != 0
    for j in range(CONV_CH // LANES):
        ls = slice(j * LANES, (j + 1) * LANES)
        xpad_ref[j, halo:halo + lb, :] = qkv_ref[0, :, ls]
        taps = [jnp.broadcast_to(convw_ref[k:k + 1, ls], (8, LANES)) for k in range(CONV_K)]
        tiles = [xpad_ref[j, pl.ds(g, 8, stride=pitch), :] for g in range(pitch)]
        wrapped = {-k: pltpu.roll(tiles[pitch - k], 1, 0) for k in range(1, CONV_K)}
        back = lambda g: tiles[g] if g >= 0 else wrapped[g]
        is_qk = j < 2 * GDN_HEADS
        scale = GDN_DK ** -0.5 if j < GDN_HEADS else 1.0
        for g in range(pitch):
            y = taps[CONV_K - 1] * tiles[g]
            for k in range(CONV_K - 1):
                y = y + taps[k] * back(g - (CONV_K - 1 - k))
            y = _silu(y)
            if is_qk:
                y = y * (lax.rsqrt(jnp.sum(y * y, axis=-1, keepdims=True) + RMS_EPS) * scale)
            act_ref[j, pl.ds(g, 8, stride=pitch), :] = y
        xpad_ref[j, 0:halo, :] = xpad_ref[j, lb:lb + halo, :]

    sm = sm_ref[0]
    beta_all = _sigmoid(sm)
    g_all = -jnp.exp(alog_ref[...]) * _softplus(sm + dtb_ref[...])

    row = _iota2((c, c), 0)
    col = _iota2((c, c), 1)
    causal = row >= col
    strict = row > col
    eye = (row == col).astype(F32)
    tril = tril_ref[...]


    g_cum = [_sum_dot(tril, g_all[rows(ci), :]) for ci in range(n_chunks)]
    g_cum_t = [jnp.transpose(jnp.concatenate([g, g], axis=0))[:, :c] for g in g_cum]

    qs, ks, vs, betas, gcs, decays, e_gs, g_lasts = {}, {}, {}, {}, {}, {}, {}, {}
    for ci, h in chains:
        rs = rows(ci)
        ars = slice(halo + ci * c, halo + (ci + 1) * c)
        qs[ci, h] = act_ref[h, ars, :]
        ks[ci, h] = act_ref[GDN_HEADS + h, ars, :]
        vs[ci, h] = act_ref[2 * GDN_HEADS + h, ars, :]
        betas[ci, h] = beta_all[rs, h:h + 1]
        gc = g_cum[ci][:, GDN_HEADS + h:GDN_HEADS + h + 1]
        gr = g_cum_t[ci][GDN_HEADS + h:GDN_HEADS + h + 1, :]
        gcs[ci, h] = gc
        g_lasts[ci, h] = gc[c - 1:c, :]
        decays[ci, h] = jnp.where(causal, jnp.exp(jnp.where(causal, gc - gr, 0.0)), 0.0)
        e_gs[ci, h] = jnp.exp(gc)

    qk_kk = {ch: _dot_nt(jnp.concatenate([qs[ch], ks[ch]], axis=0), ks[ch]) for ch in chains}
    qk = {ch: qk_kk[ch][:c] * decays[ch] for ch in chains}
    a_neg = {ch: jnp.where(strict, -(betas[ch] * qk_kk[ch][c:] * decays[ch]), 0.0) for ch in chains}
    a_bf = {ch: a_neg[ch].astype(BF16) for ch in chains}
    n_pow = {ch: _dot(a_bf[ch], a_bf[ch]).astype(BF16) for ch in chains}
    t_inv = {ch: eye + a_neg[ch] for ch in chains}
    for _ in range(4):
        y = {ch: _dot(jnp.concatenate([n_pow[ch], t_inv[ch].astype(BF16)], axis=0), n_pow[ch]) for ch in chains}
        n_pow = {ch: y[ch][:c].astype(BF16) for ch in chains}
        t_inv = {ch: t_inv[ch] + y[ch][c:] for ch in chains}
    corr = {ch: _dot(t_inv[ch], n_pow[ch]) for ch in chains}
    t_inv = {ch: t_inv[ch] + corr[ch] for ch in chains}

    sol = {ch: _dot(t_inv[ch], jnp.concatenate(
        [vs[ch] * betas[ch], ks[ch] * (betas[ch] * e_gs[ch])], axis=1)) for ch in chains}
    wq = {ch: jnp.concatenate([sol[ch][:, GDN_DV:], qs[ch] * e_gs[ch]], axis=0) for ch in chains}
    k_dec = {ch: ks[ch] * jnp.exp(g_lasts[ch] - gcs[ch]) for ch in chains}

    for ci, h in chains:
        i = ci * GDN_HEADS + h
        wq_ref[i] = wq[ci, h].astype(BF16)
        u_ref[i] = sol[ci, h][:, :GDN_DV]
        kd_ref[i] = k_dec[ci, h].astype(BF16)
        qk_ref[i] = qk[ci, h].astype(BF16)
        dl_ref[i] = jnp.broadcast_to(jnp.exp(g_lasts[ci, h]), (1, GDN_DV))


def _gdn(proj3d, conv_w, alog_row, dtb_row, norm_w, tril, *, lb):
    b, l, _ = proj3d.shape
    nb = l // lb
    n_chains = (lb // CHUNK) * GDN_HEADS
    const2 = lambda bi, ti: (0, 0)
    prep = lambda ti: jnp.minimum(ti, nb - 1)
    scan = lambda ti: jnp.maximum(ti - 1, 0)
    return pl.pallas_call(
        functools.partial(_gdn_kernel, lb=lb),
        grid=(b, nb + 1),
        in_specs=[
            pl.BlockSpec((1, lb, CONV_CH), lambda bi, ti: (bi, prep(ti), OFF_QKV // CONV_CH)),
            pl.BlockSpec((1, lb, GDN_V_W), lambda bi, ti: (bi, scan(ti), OFF_Z // GDN_V_W)),
            pl.BlockSpec((1, lb, SMALL_W), lambda bi, ti: (bi, prep(ti), OFF_SMALL // SMALL_W)),
            pl.BlockSpec((CONV_K, CONV_CH), const2),
            pl.BlockSpec((1, SMALL_W), const2),
            pl.BlockSpec((1, SMALL_W), const2),
            pl.BlockSpec((1, GDN_DV), const2),
            pl.BlockSpec((CHUNK, 3 * CHUNK), const2),
        ],
        out_specs=pl.BlockSpec((1, lb, GDN_V_W), lambda bi, ti: (bi, scan(ti), 0)),
        out_shape=jax.ShapeDtypeStruct((b, l, GDN_V_W), F32),
        scratch_shapes=[
            pltpu.VMEM((CONV_CH // LANES, lb + 8, LANES), F32),
            pltpu.VMEM((CONV_CH // LANES, lb + 8, LANES), F32),
            pltpu.VMEM((GDN_HEADS, GDN_DK, GDN_DV), F32),
            pltpu.VMEM((n_chains, 2 * CHUNK, GDN_DK), BF16),
            pltpu.VMEM((n_chains, CHUNK, GDN_DV), F32),
            pltpu.VMEM((n_chains, CHUNK, GDN_DK), BF16),
            pltpu.VMEM((n_chains, CHUNK, CHUNK), BF16),
            pltpu.VMEM((n_chains, 1, GDN_DV), F32),
        ],
        compiler_params=pltpu.CompilerParams(
            dimension_semantics=("parallel", "arbitrary"), vmem_limit_bytes=VMEM_LIMIT_BYTES),
        name="gdn",
    )(proj3d, proj3d, proj3d, conv_w, alog_row, dtb_row, norm_w, tril)


GLA_LEVELS = (32, 16, 8, 4, 2, 1)


def _gla_sum_matrix():
    c = CHUNK
    r = np.arange(c)[:, None]
    s = np.arange(c)[None, :]
    blocks = [(s <= r), (s > r)]
    for hs in GLA_LEVELS:
        pos = r % (2 * hs)
        m = r - pos + hs
        later = pos >= hs
        blocks.append(np.where(later, (s > m) & (s <= r), (s > r) & (s <= m)))
    return np.concatenate(blocks, axis=0).astype(np.float32)


def _gla_kernel(q_ref, k_ref, v_ref, g_ref, sm_ref, wgk_ref, bgk_ref, nw_ref, lmat_ref,
                o_ref, s_ref, *, lb):
    c = CHUNK

    @pl.when(pl.program_id(1) == 0)
    def _():
        s_ref[...] = jnp.zeros(s_ref.shape, F32)

    row = _iota2((c, c), 0)
    col = _iota2((c, c), 1)
    eye = row == col
    lmat = lmat_ref[...]
    nw = nw_ref[...]
    scale = GLA_DK ** -0.5

    gate_pre = _dot_split(sm_ref[0], wgk_ref[...]) + bgk_ref[...]
    logg_all = (jnp.minimum(gate_pre, 0.0) - jnp.log(1.0 + jnp.exp(-jnp.abs(gate_pre)))) / GLA_NORMALIZER

    n_chunks = lb // c
    chains = [(ci, h) for ci in range(n_chunks) for h in range(GLA_HEADS)]
    rows = lambda ci: slice(ci * c, (ci + 1) * c)
    lanes = lambda h: slice(h * GLA_DK, (h + 1) * GLA_DK)

    sums = [_sum_dot(lmat, logg_all[rows(ci), :]) for ci in range(n_chunks)]
    q_in, k_out, e_last, qm, km, qk_diag = [], [], [], [], [], []
    for ci in range(n_chunks):
        q = q_ref[0, rows(ci), :] * scale
        k = k_ref[0, rows(ci), :]
        e_b = jnp.exp(sums[ci][0:c])
        e_last.append(e_b[c - 1:c, :])
        q_in.append(q * e_b)
        k_out.append(k * jnp.exp(sums[ci][c:2 * c]))
        qm_c, km_c = [], []
        for li, hs in enumerate(GLA_LEVELS):
            e_l = jnp.exp(sums[ci][(2 + li) * c:(3 + li) * c])
            qm_c.append(q * e_l)
            km_c.append(k * e_l)
        qm.append(qm_c)
        km.append(km_c)
        qk_diag.append(q * k)

    att = {(ci, h): jnp.where(eye, jnp.sum(qk_diag[ci][:, lanes(h)], axis=-1, keepdims=True), 0.0)
           for ci, h in chains}
    for li, hs in enumerate(GLA_LEVELS):
        part = {(ci, h): _dot_nt(qm[ci][li][:, lanes(h)], km[ci][li][:, lanes(h)]) for ci, h in chains}
        keep = ((row ^ col) < 2 * hs) & ((row & hs) != 0) & ((col & hs) == 0)
        for ch in chains:
            att[ch] = att[ch] + jnp.where(keep, part[ch], 0.0)
    vs = {(ci, h): v_ref[0, rows(ci), h * GLA_DV:(h + 1) * GLA_DV] for ci, h in chains}
    intra = {ch: _dot(att[ch], vs[ch]) for ch in chains}
    upd = {(ci, h): _dot_tn(vs[ci, h], k_out[ci][:, lanes(h)]) for ci, h in chains}

    state = [s_ref[h] for h in range(GLA_HEADS)]
    for ci in range(n_chunks):
        inter = [_dot_nt(q_in[ci][:, lanes(h)], state[h]) for h in range(GLA_HEADS)]
        for h in range(GLA_HEADS):
            state[h] = state[h] * e_last[ci][:, lanes(h)] + upd[ci, h]
            gg = g_ref[0, rows(ci), h * GLA_DV:(h + 1) * GLA_DV]
            o_ref[0, rows(ci), h * GLA_DV:(h + 1) * GLA_DV] = _gated_rmsnorm(inter[h] + intra[ci, h], gg, nw)
    for h in range(GLA_HEADS):
        s_ref[h] = state[h]


def _gla(proj3d, wgk_pad, bgk_row, norm_w, lmat, *, lb):
    b, l, _ = proj3d.shape
    const2 = lambda bi, ti: (0, 0)
    return pl.pallas_call(
        functools.partial(_gla_kernel, lb=lb),
        grid=(b, l // lb),
        in_specs=[
            pl.BlockSpec((1, lb, GLA_QK_W), lambda bi, ti: (bi, ti, OFF_GQ // GLA_QK_W)),
            pl.BlockSpec((1, lb, GLA_QK_W), lambda bi, ti: (bi, ti, OFF_GK // GLA_QK_W)),
            pl.BlockSpec((1, lb, GLA_V_W), lambda bi, ti: (bi, ti, OFF_GV // GLA_V_W)),
            pl.BlockSpec((1, lb, GLA_V_W), lambda bi, ti: (bi, ti, OFF_GG // GLA_V_W)),
            pl.BlockSpec((1, lb, SMALL_W), lambda bi, ti: (bi, ti, OFF_SMALL // SMALL_W)),
            pl.BlockSpec((SMALL_W, GLA_QK_W), const2),
            pl.BlockSpec((1, GLA_QK_W), const2),
            pl.BlockSpec((1, GLA_DV), const2),
            pl.BlockSpec(lmat.shape, const2),
        ],
        out_specs=pl.BlockSpec((1, lb, GLA_V_W), lambda bi, ti: (bi, ti, 0)),
        out_shape=jax.ShapeDtypeStruct((b, l, GLA_V_W), F32),
        scratch_shapes=[pltpu.VMEM((GLA_HEADS, GLA_DV, GLA_DK), F32)],
        compiler_params=pltpu.CompilerParams(
            dimension_semantics=("parallel", "arbitrary"), vmem_limit_bytes=VMEM_LIMIT_BYTES),
        name="gla",
    )(proj3d, proj3d, proj3d, proj3d, proj3d, wgk_pad, bgk_row, norm_w, lmat)


def _regroup_w_in(w_in):
    d = w_in.shape[0]
    o = 0
    qkv = w_in[:, o:o + CONV_CH]; o += CONV_CH
    z = w_in[:, o:o + GDN_V_W]; o += GDN_V_W
    beta = w_in[:, o:o + GDN_HEADS]; o += GDN_HEADS
    dec = w_in[:, o:o + GDN_HEADS]; o += GDN_HEADS
    gq = w_in[:, o:o + GLA_QK_W]; o += GLA_QK_W
    gk = w_in[:, o:o + GLA_QK_W]; o += GLA_QK_W
    gv = w_in[:, o:o + GLA_V_W]; o += GLA_V_W
    gg = w_in[:, o:o + GLA_V_W]; o += GLA_V_W
    lr = w_in[:, o:o + GLA_RANK]
    small = jnp.concatenate(
        [beta, dec, lr, jnp.zeros((d, SMALL_W - 2 * GDN_HEADS - GLA_RANK), w_in.dtype)], axis=1)
    return jnp.concatenate([qkv, z, gq, gk, gv, gg, small], axis=1)


def _lane_row(vals, offset, width):
    row = jnp.zeros((1, width), F32)
    return row.at[0, offset:offset + vals.shape[0]].set(vals.astype(F32))


def _layer(x, p, i, *, tm, lb_gdn, lb_gla):
    bsz, l, d = x.shape
    t = bsz * l
    row = lambda a: a.reshape(1, -1).astype(F32)

    h = _ffn_ln(x.reshape(t, d), p["ffn1_w_gate"][i].astype(BF16), p["ffn1_w_up"][i].astype(BF16),
                p["ffn1_w_down"][i].astype(BF16), row(p["ln1_g"][i]), row(p["ln1_b"][i]), tm=tm)

    proj = _proj(h, _regroup_w_in(p["w_in"][i].astype(BF16)), tm=tm).reshape(bsz, l, PROJ_W)

    tril = jnp.asarray(np.tile(np.tril(np.ones((CHUNK, CHUNK), np.float32)), (1, 3)), BF16)
    o_a = _gdn(proj, p["conv_w"][i].astype(F32),
               _lane_row(p["a_log"][i], GDN_HEADS, SMALL_W), _lane_row(p["dt_bias"][i], GDN_HEADS, SMALL_W),
               row(p["gdn_norm_w"][i]), tril, lb=lb_gdn)

    wgk_pad = jnp.zeros((SMALL_W, GLA_QK_W), F32).at[2 * GDN_HEADS:2 * GDN_HEADS + GLA_RANK, :].set(
        p["w_gk"][i].astype(F32))
    o_b = _gla(proj, wgk_pad, row(p["b_gk"][i]), row(p["gla_norm_w"][i]),
               jnp.asarray(np.tile(_gla_sum_matrix(), (1, 3)), BF16), lb=lb_gla)

    w_out = p["w_out"][i].astype(BF16)
    out = _mix_ffn_ln(h, o_a.reshape(t, GDN_V_W), o_b.reshape(t, GLA_V_W), w_out[:GDN_V_W], w_out[GDN_V_W:],
                      row(p["ln2_g"][i]), row(p["ln2_b"][i]),
                      p["ffn2_w_gate"][i].astype(BF16), p["ffn2_w_up"][i].astype(BF16),
                      p["ffn2_w_down"][i].astype(BF16), row(p["ln3_g"][i]), row(p["ln3_b"][i]), tm=tm)
    return out.reshape(bsz, l, d)


def kernel(x, ffn1_w_gate, ffn1_w_up, ffn1_w_down, ln1_g, ln1_b, w_in, conv_w, a_log, dt_bias, gdn_norm_w,
           w_gk, b_gk, gla_norm_w, w_out, ln2_g, ln2_b, ffn2_w_gate, ffn2_w_up, ffn2_w_down, ln3_g, ln3_b):
    p = dict(ffn1_w_gate=ffn1_w_gate, ffn1_w_up=ffn1_w_up, ffn1_w_down=ffn1_w_down, ln1_g=ln1_g, ln1_b=ln1_b,
             w_in=w_in, conv_w=conv_w, a_log=a_log, dt_bias=dt_bias, gdn_norm_w=gdn_norm_w,
             w_gk=w_gk, b_gk=b_gk, gla_norm_w=gla_norm_w, w_out=w_out, ln2_g=ln2_g, ln2_b=ln2_b,
             ffn2_w_gate=ffn2_w_gate, ffn2_w_up=ffn2_w_up, ffn2_w_down=ffn2_w_down, ln3_g=ln3_g, ln3_b=ln3_b)
    bsz, l, _ = x.shape
    tm = min(1024, bsz * l)
    for i in range(ffn1_w_gate.shape[0]):
        x = _layer(x, p, i, tm=tm, lb_gdn=min(256, l), lb_gla=min(512, l))
    return x
```

```python
import functools

import numpy as np
import jax
import jax.numpy as jnp
from jax import lax
from jax.experimental import pallas as pl
from jax.experimental.pallas import tpu as pltpu

D_MODEL = 1024
GDN_HEADS = 4
GDN_DK = 128
GDN_DV = 128
CONV_K = 4
GLA_HEADS = 4
GLA_DK = 64
GLA_DV = 128
GLA_RANK = 16
GLA_NORMALIZER = 16.0
CHUNK = 64
D_FF = 2816
DEPTH = 1
ALPHA = (2.0 * DEPTH) ** 0.25
LN_EPS = 1e-5
RMS_EPS = 1e-6

GDN_QK_W = GDN_HEADS * GDN_DK
GDN_V_W = GDN_HEADS * GDN_DV
GLA_QK_W = GLA_HEADS * GLA_DK
GLA_V_W = GLA_HEADS * GLA_DV
CONV_CH = 2 * GDN_QK_W + GDN_V_W

LANES = 128
MXU_DIM = 256
VMEM_LIMIT_BYTES = 56 * 1024 * 1024

SMALL_W = LANES
PROJ_W = CONV_CH + GDN_V_W + 2 * GLA_QK_W + 2 * GLA_V_W + SMALL_W
OFF_QKV = 0
OFF_Z = CONV_CH
OFF_GQ = OFF_Z + GDN_V_W
OFF_GK = OFF_GQ + GLA_QK_W
OFF_GV = OFF_GK + GLA_QK_W
OFF_GG = OFF_GV + GLA_V_W
OFF_SMALL = OFF_GG + GLA_V_W

BF16 = jnp.bfloat16
F32 = jnp.float32


def _dot(a, b):
    return jnp.dot(a.astype(BF16), b.astype(BF16), preferred_element_type=F32)


def _dot_nt(a, b):
    return lax.dot_general(a.astype(BF16), b.astype(BF16), (((1,), (1,)), ((), ())),
                           preferred_element_type=F32)


def _dot_tn(a, b):
    return lax.dot_general(a.astype(BF16), b.astype(BF16), (((0,), (0,)), ((), ())),
                           preferred_element_type=F32)


def _split_bf16(x, pieces):
    out = []
    for _ in range(pieces - 1):
        p = x.astype(BF16)
        out.append(p)
        x = x - p.astype(F32)
    out.append(x.astype(BF16))
    return out


def _sum_dot(sel3, x):
    return jnp.dot(sel3, jnp.concatenate(_split_bf16(x, 3), axis=0), preferred_element_type=F32)


def _dot_split(a, b):
    a_hi, a_lo = _split_bf16(a, 2)
    b_hi, b_lo = _split_bf16(b, 2)
    return jnp.dot(jnp.concatenate([a_hi, a_lo, a_hi], axis=1), jnp.concatenate([b_hi, b_hi, b_lo], axis=0),
                   preferred_element_type=F32)


NEG_LOG2_E = -1.4426950408889634


def _sigmoid(x):
    return 1.0 / (1.0 + jnp.exp2(x * NEG_LOG2_E))


def _silu(x):
    return x * _sigmoid(x)


def _softplus(x):
    return jnp.maximum(x, 0.0) + jnp.log(1.0 + jnp.exp(-jnp.abs(x)))


def _layer_norm_rows(r, g, b):
    mu = jnp.mean(r, axis=-1, keepdims=True)
    c = r - mu
    var = jnp.mean(c * c, axis=-1, keepdims=True)
    return c * lax.rsqrt(var + LN_EPS) * g + b


ROW_SPLITS = 4


def _row_groups(tm):
    step = tm // ROW_SPLITS
    return [slice(r * step, (r + 1) * step) for r in range(ROW_SPLITS)]


def _swiglu_ln(x, wg_ref, wu_ref, wd_ref, g_ref, b_ref, act_ref, o_ref, ff_chunk):
    xb = x.astype(BF16)
    d_ff = wg_ref.shape[1]
    for j in range(d_ff // ff_chunk):
        sl = slice(j * ff_chunk, (j + 1) * ff_chunk)
        gate = jnp.dot(xb, wg_ref[:, sl], preferred_element_type=F32)
        up = jnp.dot(xb, wu_ref[:, sl], preferred_element_type=F32)
        act_ref[:, sl] = (_silu(gate) * up).astype(BF16)
    for rs in _row_groups(x.shape[0]):
        y = jnp.dot(act_ref[rs, :], wd_ref[...], preferred_element_type=F32)
        o_ref[rs, :] = _layer_norm_rows(ALPHA * x[rs, :] + 0.5 * y, g_ref[...], b_ref[...])


def _ffn_ln_kernel(x_ref, wg_ref, wu_ref, wd_ref, g_ref, b_ref, o_ref, act_ref, *, ff_chunk):
    _swiglu_ln(x_ref[...], wg_ref, wu_ref, wd_ref, g_ref, b_ref, act_ref, o_ref, ff_chunk)


def _mix_ffn_ln_kernel(h_ref, oa_ref, ob_ref, wa_ref, wb_ref, g2_ref, b2_ref,
                       wg_ref, wu_ref, wd_ref, g3_ref, b3_ref, o_ref, act_ref, *, ff_chunk):
    xs = []
    for rs in _row_groups(h_ref.shape[0]):
        mix = (jnp.dot(oa_ref[rs, :].astype(BF16), wa_ref[...], preferred_element_type=F32)
               + jnp.dot(ob_ref[rs, :].astype(BF16), wb_ref[...], preferred_element_type=F32))
        xs.append(_layer_norm_rows(ALPHA * h_ref[rs, :] + mix, g2_ref[...], b2_ref[...]))
    _swiglu_ln(jnp.concatenate(xs, axis=0), wg_ref, wu_ref, wd_ref, g3_ref, b3_ref, act_ref, o_ref, ff_chunk)


def _mix_ffn_ln(h2d, oa2d, ob2d, w_a, w_b, ln2_g, ln2_b, wg, wu, wd, ln3_g, ln3_b, *, tm):
    t, d = h2d.shape
    d_ff = wg.shape[1]
    const = lambda i: (0, 0)
    tile = lambda w: pl.BlockSpec((tm, w), lambda i: (i, 0))
    resident = lambda a: pl.BlockSpec(a.shape, const, pipeline_mode=pl.Buffered(1))
    return pl.pallas_call(
        functools.partial(_mix_ffn_ln_kernel, ff_chunk=MXU_DIM),
        grid=(t // tm,),
        in_specs=[tile(d), tile(oa2d.shape[1]), tile(ob2d.shape[1]), resident(w_a), resident(w_b),
                  pl.BlockSpec((1, d), const), pl.BlockSpec((1, d), const),
                  resident(wg), resident(wu), resident(wd),
                  pl.BlockSpec((1, d), const), pl.BlockSpec((1, d), const)],
        out_specs=tile(d),
        out_shape=jax.ShapeDtypeStruct((t, d), F32),
        scratch_shapes=[pltpu.VMEM((tm, d_ff), BF16)],
        compiler_params=pltpu.CompilerParams(
            dimension_semantics=("parallel",), vmem_limit_bytes=VMEM_LIMIT_BYTES),
        name="mix_ffn_ln",
    )(h2d, oa2d, ob2d, w_a, w_b, ln2_g, ln2_b, wg, wu, wd, ln3_g, ln3_b)


def _ffn_ln(x2d, wg, wu, wd, ln_g, ln_b, *, tm):
    t, d = x2d.shape
    d_ff = wg.shape[1]
    const = lambda i: (0, 0)
    return pl.pallas_call(
        functools.partial(_ffn_ln_kernel, ff_chunk=MXU_DIM),
        grid=(t // tm,),
        in_specs=[
            pl.BlockSpec((tm, d), lambda i: (i, 0)),
            pl.BlockSpec((d, d_ff), const, pipeline_mode=pl.Buffered(1)),
            pl.BlockSpec((d, d_ff), const, pipeline_mode=pl.Buffered(1)),
            pl.BlockSpec((d_ff, d), const, pipeline_mode=pl.Buffered(1)),
            pl.BlockSpec((1, d), const),
            pl.BlockSpec((1, d), const),
        ],
        out_specs=pl.BlockSpec((tm, d), lambda i: (i, 0)),
        out_shape=jax.ShapeDtypeStruct((t, d), F32),
        scratch_shapes=[pltpu.VMEM((tm, d_ff), BF16)],
        compiler_params=pltpu.CompilerParams(
            dimension_semantics=("parallel",), vmem_limit_bytes=VMEM_LIMIT_BYTES),
        name="ffn_ln",
    )(x2d, wg, wu, wd, ln_g, ln_b)


def _proj_kernel(h_ref, w_ref, o_ref):
    o_ref[...] = jnp.dot(h_ref[...].astype(BF16), w_ref[...], preferred_element_type=F32)


def _proj(h2d, w, *, tm):
    t, d = h2d.shape
    n = w.shape[1]
    return pl.pallas_call(
        _proj_kernel,
        grid=(t // tm,),
        in_specs=[
            pl.BlockSpec((tm, d), lambda i: (i, 0)),
            pl.BlockSpec((d, n), lambda i: (0, 0), pipeline_mode=pl.Buffered(1)),
        ],
        out_specs=pl.BlockSpec((tm, n), lambda i: (i, 0)),
        out_shape=jax.ShapeDtypeStruct((t, n), F32),
        compiler_params=pltpu.CompilerParams(
            dimension_semantics=("parallel",), vmem_limit_bytes=VMEM_LIMIT_BYTES),
        name="proj",
    )(h2d, w)


def _iota2(shape, dim):
    return lax.broadcasted_iota(jnp.int32, shape, dim)


def _gated_rmsnorm(o, gate, w):
    ms = jnp.mean(o * o, axis=-1, keepdims=True)
    return o * lax.rsqrt(ms + RMS_EPS) * w * _silu(gate)


GDN_ROWS_PER_STEP = 4


def _gdn_kernel(qkv_ref, z_ref, sm_ref, convw_ref, alog_ref, dtb_ref, nw_ref, tril_ref,
                o_ref, xpad_ref, act_ref, s_ref, wq_ref, u_ref, kd_ref, qk_ref, dl_ref, *, lb):
    c = CHUNK
    halo = 8
    n_chunks = lb // c
    n_rows = qkv_ref.shape[0]
    n_slabs = CONV_CH // LANES
    n_lanes = n_rows * GDN_HEADS
    lanes = [(r, h) for r in range(n_rows) for h in range(GDN_HEADS)]
    chains = [(ci, ln) for ci in range(n_chunks) for ln in range(n_lanes)]
    rows = lambda ci: slice(ci * c, (ci + 1) * c)
    nw = nw_ref[...]

    @pl.when(pl.program_id(1) == 0)
    def _():
        xpad_ref[:, 0:halo, :] = jnp.zeros((n_rows * n_slabs, halo, LANES), F32)
        for ref in (s_ref, wq_ref, u_ref, kd_ref, qk_ref, dl_ref):
            ref[...] = jnp.zeros(ref.shape, ref.dtype)

    state = [s_ref[ln] for ln in range(n_lanes)]
    for ci in range(n_chunks):
        idx = [ci * n_lanes + ln for ln in range(n_lanes)]
        ws = [_dot(wq_ref[i], state[ln]) for ln, i in enumerate(idx)]
        v_new = [u_ref[i] - ws[ln][:c] for ln, i in enumerate(idx)]
        upd = [_dot_tn(kd_ref[i], v_new[ln]) for ln, i in enumerate(idx)]
        intra = [_dot(qk_ref[i], v_new[ln]) for ln, i in enumerate(idx)]
        for ln, i in enumerate(idx):
            r, h = lanes[ln]
            state[ln] = state[ln] * dl_ref[i] + upd[ln]
            zg = z_ref[r, rows(ci), h * GDN_DV:(h + 1) * GDN_DV]
            o_ref[r, rows(ci), h * GDN_DV:(h + 1) * GDN_DV] = _gated_rmsnorm(ws[ln][c:] + intra[ln], zg, nw)
    for ln in range(n_lanes):
        s_ref[ln] = state[ln]

    pitch = (halo + lb) // 8
    assert pitch * 8 == halo + lb and pitch % 8 != 0
    for r in range(n_rows):
        for j in range(n_slabs):
            sj = r * n_slabs + j
            ls = slice(j * LANES, (j + 1) * LANES)
            xpad_ref[sj, halo:halo + lb, :] = qkv_ref[r, :, ls]
            taps = [jnp.broadcast_to(convw_ref[k:k + 1, ls], (8, LANES)) for k in range(CONV_K)]
            tiles = [xpad_ref[sj, pl.ds(g, 8, stride=pitch), :] for g in range(pitch)]
            wrapped = {-k: pltpu.roll(tiles[pitch - k], 1, 0) for k in range(1, CONV_K)}
            back = lambda g: tiles[g] if g >= 0 else wrapped[g]
            is_qk = j < 2 * GDN_HEADS
            scale = GDN_DK ** -0.5 if j < GDN_HEADS else 1.0
            for g in range(pitch):
                y = taps[CONV_K - 1] * tiles[g]
                for k in range(CONV_K - 1):
                    y = y + taps[k] * back(g - (CONV_K - 1 - k))
                y = _silu(y)
                if is_qk:
                    y = y * (lax.rsqrt(jnp.sum(y * y, axis=-1, keepdims=True) + RMS_EPS) * scale)
                act_ref[sj, pl.ds(g, 8, stride=pitch), :] = y
            xpad_ref[sj, 0:halo, :] = xpad_ref[sj, lb:lb + halo, :]

    row = _iota2((c, c), 0)
    col = _iota2((c, c), 1)
    causal = row >= col
    strict = row > col
    eye = (row == col).astype(F32)
    tril = tril_ref[...]

    beta_all, g_cum, g_cum_t = [], [], []
    for r in range(n_rows):
        sm = sm_ref[r]
        beta_all.append(_sigmoid(sm))
        g_all = -jnp.exp(alog_ref[...]) * _softplus(sm + dtb_ref[...])
        g_cum.append([_sum_dot(tril, g_all[rows(ci), :]) for ci in range(n_chunks)])
        g_cum_t.append([jnp.transpose(jnp.concatenate([g, g], axis=0))[:, :c] for g in g_cum[r]])

    qs, ks, vs, betas, gcs, decays, e_gs, g_lasts = {}, {}, {}, {}, {}, {}, {}, {}
    for ci, ln in chains:
        r, h = lanes[ln]
        rs = rows(ci)
        ars = slice(halo + ci * c, halo + (ci + 1) * c)
        qs[ci, ln] = act_ref[r * n_slabs + h, ars, :]
        ks[ci, ln] = act_ref[r * n_slabs + GDN_HEADS + h, ars, :]
        vs[ci, ln] = act_ref[r * n_slabs + 2 * GDN_HEADS + h, ars, :]
        betas[ci, ln] = beta_all[r][rs, h:h + 1]
        gc = g_cum[r][ci][:, GDN_HEADS + h:GDN_HEADS + h + 1]
        gr = g_cum_t[r][ci][GDN_HEADS + h:GDN_HEADS + h + 1, :]
        gcs[ci, ln] = gc
        g_lasts[ci, ln] = gc[c - 1:c, :]
        decays[ci, ln] = jnp.where(causal, jnp.exp(jnp.where(causal, gc - gr, 0.0)), 0.0)
        e_gs[ci, ln] = jnp.exp(gc)

    qk_kk = {ch: _dot_nt(jnp.concatenate([qs[ch], ks[ch]], axis=0), ks[ch]) for ch in chains}
    qk = {ch: qk_kk[ch][:c] * decays[ch] for ch in chains}
    a_neg = {ch: jnp.where(strict, -(betas[ch] * qk_kk[ch][c:] * decays[ch]), 0.0) for ch in chains}
    a_bf = {ch: a_neg[ch].astype(BF16) for ch in chains}
    n_pow = {ch: _dot(a_bf[ch], a_bf[ch]).astype(BF16) for ch in chains}
    t_inv = {ch: eye + a_neg[ch] for ch in chains}
    for _ in range(4):
        y = {ch: _dot(jnp.concatenate([n_pow[ch], t_inv[ch].astype(BF16)], axis=0), n_pow[ch]) for ch in chains}
        n_pow = {ch: y[ch][:c].astype(BF16) for ch in chains}
        t_inv = {ch: t_inv[ch] + y[ch][c:] for ch in chains}
    corr = {ch: _dot(t_inv[ch], n_pow[ch]) for ch in chains}
    t_inv = {ch: t_inv[ch] + corr[ch] for ch in chains}

    sol = {ch: _dot(t_inv[ch], jnp.concatenate(
        [vs[ch] * betas[ch], ks[ch] * (betas[ch] * e_gs[ch])], axis=1)) for ch in chains}
    wq = {ch: jnp.concatenate([sol[ch][:, GDN_DV:], qs[ch] * e_gs[ch]], axis=0) for ch in chains}
    k_dec = {ch: ks[ch] * jnp.exp(g_lasts[ch] - gcs[ch]) for ch in chains}

    for ci, ln in chains:
        i = ci * n_lanes + ln
        wq_ref[i] = wq[ci, ln].astype(BF16)
        u_ref[i] = sol[ci, ln][:, :GDN_DV]
        kd_ref[i] = k_dec[ci, ln].astype(BF16)
        qk_ref[i] = qk[ci, ln].astype(BF16)
        dl_ref[i] = jnp.broadcast_to(jnp.exp(g_lasts[ci, ln]), (1, GDN_DV))


def _gdn(proj3d, conv_w, alog_row, dtb_row, norm_w, tril, *, lb):
    b, l, _ = proj3d.shape
    nb = l // lb
    nr = GDN_ROWS_PER_STEP if b % GDN_ROWS_PER_STEP == 0 else 1
    n_chains = (lb // CHUNK) * GDN_HEADS * nr
    n_slabs = CONV_CH // LANES
    const2 = lambda bi, ti: (0, 0)
    prep = lambda ti: jnp.minimum(ti, nb - 1)
    scan = lambda ti: jnp.maximum(ti - 1, 0)
    return pl.pallas_call(
        functools.partial(_gdn_kernel, lb=lb),
        grid=(b // nr, nb + 1),
        in_specs=[
            pl.BlockSpec((nr, lb, CONV_CH), lambda bi, ti: (bi, prep(ti), OFF_QKV // CONV_CH)),
            pl.BlockSpec((nr, lb, GDN_V_W), lambda bi, ti: (bi, scan(ti), OFF_Z // GDN_V_W)),
            pl.BlockSpec((nr, lb, SMALL_W), lambda bi, ti: (bi, prep(ti), OFF_SMALL // SMALL_W)),
            pl.BlockSpec((CONV_K, CONV_CH), const2),
            pl.BlockSpec((1, SMALL_W), const2),
            pl.BlockSpec((1, SMALL_W), const2),
            pl.BlockSpec((1, GDN_DV), const2),
            pl.BlockSpec((CHUNK, 3 * CHUNK), const2),
        ],
        out_specs=pl.BlockSpec((nr, lb, GDN_V_W), lambda bi, ti: (bi, scan(ti), 0)),
        out_shape=jax.ShapeDtypeStruct((b, l, GDN_V_W), F32),
        scratch_shapes=[
            pltpu.VMEM((nr * n_slabs, lb + 8, LANES), F32),
            pltpu.VMEM((nr * n_slabs, lb + 8, LANES), F32),
            pltpu.VMEM((nr * GDN_HEADS, GDN_DK, GDN_DV), F32),
            pltpu.VMEM((n_chains, 2 * CHUNK, GDN_DK), BF16),
            pltpu.VMEM((n_chains, CHUNK, GDN_DV), F32),
            pltpu.VMEM((n_chains, CHUNK, GDN_DK), BF16),
            pltpu.VMEM((n_chains, CHUNK, CHUNK), BF16),
            pltpu.VMEM((n_chains, 1, GDN_DV), F32),
        ],
        compiler_params=pltpu.CompilerParams(
            dimension_semantics=("parallel", "arbitrary"), vmem_limit_bytes=VMEM_LIMIT_BYTES),
        name="gdn",
    )(proj3d, proj3d, proj3d, conv_w, alog_row, dtb_row, norm_w, tril)


GLA_LEVELS = (32, 16, 8, 4, 2, 1)


def _gla_sum_matrix():
    c = CHUNK
    r = np.arange(c)[:, None]
    s = np.arange(c)[None, :]
    blocks = [(s <= r), (s > r)]
    for hs in GLA_LEVELS:
        pos = r % (2 * hs)
        m = r - pos + hs
        later = pos >= hs
        blocks.append(np.where(later, (s > m) & (s <= r), (s > r) & (s <= m)))
    return np.concatenate(blocks, axis=0).astype(np.float32)


def _gla_kernel(q_ref, k_ref, v_ref, g_ref, sm_ref, wgk_ref, bgk_ref, nw_ref, lmat_ref,
                o_ref, s_ref, *, lb):
    c = CHUNK

    @pl.when(pl.program_id(1) == 0)
    def _():
        s_ref[...] = jnp.zeros(s_ref.shape, F32)

    row = _iota2((c, c), 0)
    col = _iota2((c, c), 1)
    eye = row == col
    lmat = lmat_ref[...]
    nw = nw_ref[...]
    scale = GLA_DK ** -0.5

    gate_pre = _dot_split(sm_ref[0], wgk_ref[...]) + bgk_ref[...]
    logg_all = (jnp.minimum(gate_pre, 0.0) - jnp.log(1.0 + jnp.exp(-jnp.abs(gate_pre)))) / GLA_NORMALIZER

    n_chunks = lb // c
    chains = [(ci, h) for ci in range(n_chunks) for h in range(GLA_HEADS)]
    rows = lambda ci: slice(ci * c, (ci + 1) * c)
    lanes = lambda h: slice(h * GLA_DK, (h + 1) * GLA_DK)

    sums = [_sum_dot(lmat, logg_all[rows(ci), :]) for ci in range(n_chunks)]
    q_in, k_out, e_last, qm, km, qk_diag = [], [], [], [], [], []
    for ci in range(n_chunks):
        q = q_ref[0, rows(ci), :] * scale
        k = k_ref[0, rows(ci), :]
        e_b = jnp.exp(sums[ci][0:c])
        e_last.append(e_b[c - 1:c, :])
        q_in.append(q * e_b)
        k_out.append(k * jnp.exp(sums[ci][c:2 * c]))
        qm_c, km_c = [], []
        for li, hs in enumerate(GLA_LEVELS):
            e_l = jnp.exp(sums[ci][(2 + li) * c:(3 + li) * c])
            qm_c.append(q * e_l)
            km_c.append(k * e_l)
        qm.append(qm_c)
        km.append(km_c)
        qk_diag.append(q * k)

    att = {(ci, h): jnp.where(eye, jnp.sum(qk_diag[ci][:, lanes(h)], axis=-1, keepdims=True), 0.0)
           for ci, h in chains}
    for li, hs in enumerate(GLA_LEVELS):
        part = {(ci, h): _dot_nt(qm[ci][li][:, lanes(h)], km[ci][li][:, lanes(h)]) for ci, h in chains}
        keep = ((row ^ col) < 2 * hs) & ((row & hs) != 0) & ((col & hs) == 0)
        for ch in chains:
            att[ch] = att[ch] + jnp.where(keep, part[ch], 0.0)
    vs = {(ci, h): v_ref[0, rows(ci), h * GLA_DV:(h + 1) * GLA_DV] for ci, h in chains}
    intra = {ch: _dot(att[ch], vs[ch]) for ch in chains}
    upd = {(ci, h): _dot_tn(vs[ci, h], k_out[ci][:, lanes(h)]) for ci, h in chains}

    state = [s_ref[h] for h in range(GLA_HEADS)]
    for ci in range(n_chunks):
        inter = [_dot_nt(q_in[ci][:, lanes(h)], state[h]) for h in range(GLA_HEADS)]
        for h in range(GLA_HEADS):
            state[h] = state[h] * e_last[ci][:, lanes(h)] + upd[ci, h]
            gg = g_ref[0, rows(ci), h * GLA_DV:(h + 1) * GLA_DV]
            o_ref[0, rows(ci), h * GLA_DV:(h + 1) * GLA_DV] = _gated_rmsnorm(inter[h] + intra[ci, h], gg, nw)
    for h in range(GLA_HEADS):
        s_ref[h] = state[h]


def _gla(proj3d, wgk_pad, bgk_row, norm_w, lmat, *, lb):
    b, l, _ = proj3d.shape
    const2 = lambda bi, ti: (0, 0)
    return pl.pallas_call(
        functools.partial(_gla_kernel, lb=lb),
        grid=(b, l // lb),
        in_specs=[
            pl.BlockSpec((1, lb, GLA_QK_W), lambda bi, ti: (bi, ti, OFF_GQ // GLA_QK_W)),
            pl.BlockSpec((1, lb, GLA_QK_W), lambda bi, ti: (bi, ti, OFF_GK // GLA_QK_W)),
            pl.BlockSpec((1, lb, GLA_V_W), lambda bi, ti: (bi, ti, OFF_GV // GLA_V_W)),
            pl.BlockSpec((1, lb, GLA_V_W), lambda bi, ti: (bi, ti, OFF_GG // GLA_V_W)),
            pl.BlockSpec((1, lb, SMALL_W), lambda bi, ti: (bi, ti, OFF_SMALL // SMALL_W)),
            pl.BlockSpec((SMALL_W, GLA_QK_W), const2),
            pl.BlockSpec((1, GLA_QK_W), const2),
            pl.BlockSpec((1, GLA_DV), const2),
            pl.BlockSpec(lmat.shape, const2),
        ],
        out_specs=pl.BlockSpec((1, lb, GLA_V_W), lambda bi, ti: (bi, ti, 0)),
        out_shape=jax.ShapeDtypeStruct((b, l, GLA_V_W), F32),
        scratch_shapes=[pltpu.VMEM((GLA_HEADS, GLA_DV, GLA_DK), F32)],
        compiler_params=pltpu.CompilerParams(
            dimension_semantics=("parallel", "arbitrary"), vmem_limit_bytes=VMEM_LIMIT_BYTES),
        name="gla",
    )(proj3d, proj3d, proj3d, proj3d, proj3d, wgk_pad, bgk_row, norm_w, lmat)


def _regroup_w_in(w_in):
    d = w_in.shape[0]
    o = 0
    qkv = w_in[:, o:o + CONV_CH]; o += CONV_CH
    z = w_in[:, o:o + GDN_V_W]; o += GDN_V_W
    beta = w_in[:, o:o + GDN_HEADS]; o += GDN_HEADS
    dec = w_in[:, o:o + GDN_HEADS]; o += GDN_HEADS
    gq = w_in[:, o:o + GLA_QK_W]; o += GLA_QK_W
    gk = w_in[:, o:o + GLA_QK_W]; o += GLA_QK_W
    gv = w_in[:, o:o + GLA_V_W]; o += GLA_V_W
    gg = w_in[:, o:o + GLA_V_W]; o += GLA_V_W
    lr = w_in[:, o:o + GLA_RANK]
    small = jnp.concatenate(
        [beta, dec, lr, jnp.zeros((d, SMALL_W - 2 * GDN_HEADS - GLA_RANK), w_in.dtype)], axis=1)
    return jnp.concatenate([qkv, z, gq, gk, gv, gg, small], axis=1)


def _lane_row(vals, offset, width):
    row = jnp.zeros((1, width), F32)
    return row.at[0, offset:offset + vals.shape[0]].set(vals.astype(F32))


def _layer(x, p, i, *, tm, lb_gdn, lb_gla):
    bsz, l, d = x.shape
    t = bsz * l
    row = lambda a: a.reshape(1, -1).astype(F32)

    h = _ffn_ln(x.reshape(t, d), p["ffn1_w_gate"][i].astype(BF16), p["ffn1_w_up"][i].astype(BF16),
                p["ffn1_w_down"][i].astype(BF16), row(p["ln1_g"][i]), row(p["ln1_b"][i]), tm=tm)

    proj = _proj(h, _regroup_w_in(p["w_in"][i].astype(BF16)), tm=tm).reshape(bsz, l, PROJ_W)

    tril = jnp.asarray(np.tile(np.tril(np.ones((CHUNK, CHUNK), np.float32)), (1, 3)), BF16)
    o_a = _gdn(proj, p["conv_w"][i].astype(F32),
               _lane_row(p["a_log"][i], GDN_HEADS, SMALL_W), _lane_row(p["dt_bias"][i], GDN_HEADS, SMALL_W),
               row(p["gdn_norm_w"][i]), tril, lb=lb_gdn)

    wgk_pad = jnp.zeros((SMALL_W, GLA_QK_W), F32).at[2 * GDN_HEADS:2 * GDN_HEADS + GLA_RANK, :].set(
        p["w_gk"][i].astype(F32))
    o_b = _gla(proj, wgk_pad, row(p["b_gk"][i]), row(p["gla_norm_w"][i]),
               jnp.asarray(np.tile(_gla_sum_matrix(), (1, 3)), BF16), lb=lb_gla)

    w_out = p["w_out"][i].astype(BF16)
    out = _mix_ffn_ln(h, o_a.reshape(t, GDN_V_W), o_b.reshape(t, GLA_V_W), w_out[:GDN_V_W], w_out[GDN_V_W:],
                      row(p["ln2_g"][i]), row(p["ln2_b"][i]),
                      p["ffn2_w_gate"][i].astype(BF16), p["ffn2_w_up"][i].astype(BF16),
                      p["ffn2_w_down"][i].astype(BF16), row(p["ln3_g"][i]), row(p["ln3_b"][i]), tm=tm)
    return out.reshape(bsz, l, d)


def kernel(x, ffn1_w_gate, ffn1_w_up, ffn1_w_down, ln1_g, ln1_b, w_in, conv_w, a_log, dt_bias, gdn_norm_w,
           w_gk, b_gk, gla_norm_w, w_out, ln2_g, ln2_b, ffn2_w_gate, ffn2_w_up, ffn2_w_down, ln3_g, ln3_b):
    p = dict(ffn1_w_gate=ffn1_w_gate, ffn1_w_up=ffn1_w_up, ffn1_w_down=ffn1_w_down, ln1_g=ln1_g, ln1_b=ln1_b,
             w_in=w_in, conv_w=conv_w, a_log=a_log, dt_bias=dt_bias, gdn_norm_w=gdn_norm_w,
             w_gk=w_gk, b_gk=b_gk, gla_norm_w=gla_norm_w, w_out=w_out, ln2_g=ln2_g, ln2_b=ln2_b,
             ffn2_w_gate=ffn2_w_gate, ffn2_w_up=ffn2_w_up, ffn2_w_down=ffn2_w_down, ln3_g=ln3_g, ln3_b=ln3_b)
    bsz, l, _ = x.shape
    tm = min(1024, bsz * l)
    for i in range(ffn1_w_gate.shape[0]):
        x = _layer(x, p, i, tm=tm, lb_gdn=min(256, l), lb_gla=min(1024, l))
    return x
```

```python
import functools

import numpy as np
import jax
import jax.numpy as jnp
from jax import lax
from jax.experimental import pallas as pl
from jax.experimental.pallas import tpu as pltpu

D_MODEL = 1024
GDN_HEADS = 4
GDN_DK = 128
GDN_DV = 128
CONV_K = 4
GLA_HEADS = 4
GLA_DK = 64
GLA_DV = 128
GLA_RANK = 16
GLA_NORMALIZER = 16.0
CHUNK = 64
D_FF = 2816
DEPTH = 1
ALPHA = (2.0 * DEPTH) ** 0.25
LN_EPS = 1e-5
RMS_EPS = 1e-6

GDN_QK_W = GDN_HEADS * GDN_DK
GDN_V_W = GDN_HEADS * GDN_DV
GLA_QK_W = GLA_HEADS * GLA_DK
GLA_V_W = GLA_HEADS * GLA_DV
CONV_CH = 2 * GDN_QK_W + GDN_V_W

LANES = 128
MXU_DIM = 256
VMEM_LIMIT_BYTES = 56 * 1024 * 1024

SMALL_W = LANES
PROJ_W = CONV_CH + GDN_V_W + 2 * GLA_QK_W + 2 * GLA_V_W + SMALL_W
OFF_QKV = 0
OFF_Z = CONV_CH
OFF_GQ = OFF_Z + GDN_V_W
OFF_GK = OFF_GQ + GLA_QK_W
OFF_GV = OFF_GK + GLA_QK_W
OFF_GG = OFF_GV + GLA_V_W
OFF_SMALL = OFF_GG + GLA_V_W

BF16 = jnp.bfloat16
F32 = jnp.float32


def _dot(a, b):
    return jnp.dot(a.astype(BF16), b.astype(BF16), preferred_element_type=F32)


def _dot_nt(a, b):
    return lax.dot_general(a.astype(BF16), b.astype(BF16), (((1,), (1,)), ((), ())),
                           preferred_element_type=F32)


def _dot_tn(a, b):
    return lax.dot_general(a.astype(BF16), b.astype(BF16), (((0,), (0,)), ((), ())),
                           preferred_element_type=F32)


def _split_bf16(x, pieces):
    out = []
    for _ in range(pieces - 1):
        p = x.astype(BF16)
        out.append(p)
        x = x - p.astype(F32)
    out.append(x.astype(BF16))
    return out


def _sum_dot(sel3, x):
    return jnp.dot(sel3, jnp.concatenate(_split_bf16(x, 3), axis=0), preferred_element_type=F32)


def _dot_split(a, b):
    a_hi, a_lo = _split_bf16(a, 2)
    b_hi, b_lo = _split_bf16(b, 2)
    return jnp.dot(jnp.concatenate([a_hi, a_lo, a_hi], axis=1), jnp.concatenate([b_hi, b_hi, b_lo], axis=0),
                   preferred_element_type=F32)


NEG_LOG2_E = -1.4426950408889634


def _sigmoid(x):
    return 1.0 / (1.0 + jnp.exp2(x * NEG_LOG2_E))


def _silu(x):
    return x * _sigmoid(x)


def _softplus(x):
    return jnp.maximum(x, 0.0) + jnp.log(1.0 + jnp.exp(-jnp.abs(x)))


def _layer_norm_rows(r, g, b):
    mu = jnp.mean(r, axis=-1, keepdims=True)
    c = r - mu
    var = jnp.mean(c * c, axis=-1, keepdims=True)
    return c * lax.rsqrt(var + LN_EPS) * g + b


ROW_SPLITS = 4


def _row_groups(tm):
    step = tm // ROW_SPLITS
    return [slice(r * step, (r + 1) * step) for r in range(ROW_SPLITS)]


def _swiglu_ln(x, wg_ref, wu_ref, wd_ref, g_ref, b_ref, act_ref, o_ref, ff_chunk):
    xb = x.astype(BF16)
    d_ff = wg_ref.shape[1]
    for j in range(d_ff // ff_chunk):
        sl = slice(j * ff_chunk, (j + 1) * ff_chunk)
        gate = jnp.dot(xb, wg_ref[:, sl], preferred_element_type=F32)
        up = jnp.dot(xb, wu_ref[:, sl], preferred_element_type=F32)
        act_ref[:, sl] = (_silu(gate) * up).astype(BF16)
    for rs in _row_groups(x.shape[0]):
        y = jnp.dot(act_ref[rs, :], wd_ref[...], preferred_element_type=F32)
        o_ref[rs, :] = _layer_norm_rows(ALPHA * x[rs, :] + 0.5 * y, g_ref[...], b_ref[...])


def _ffn_ln_kernel(x_ref, wg_ref, wu_ref, wd_ref, g_ref, b_ref, o_ref, act_ref, *, ff_chunk):
    _swiglu_ln(x_ref[...], wg_ref, wu_ref, wd_ref, g_ref, b_ref, act_ref, o_ref, ff_chunk)


def _mix_ffn_ln_kernel(h_ref, oa_ref, ob_ref, wa_ref, wb_ref, g2_ref, b2_ref,
                       wg_ref, wu_ref, wd_ref, g3_ref, b3_ref, o_ref, act_ref, *, ff_chunk):
    xs = []
    for rs in _row_groups(h_ref.shape[0]):
        mix = (jnp.dot(oa_ref[rs, :].astype(BF16), wa_ref[...], preferred_element_type=F32)
               + jnp.dot(ob_ref[rs, :].astype(BF16), wb_ref[...], preferred_element_type=F32))
        xs.append(_layer_norm_rows(ALPHA * h_ref[rs, :] + mix, g2_ref[...], b2_ref[...]))
    _swiglu_ln(jnp.concatenate(xs, axis=0), wg_ref, wu_ref, wd_ref, g3_ref, b3_ref, act_ref, o_ref, ff_chunk)


def _mix_ffn_ln(h2d, oa2d, ob2d, w_a, w_b, ln2_g, ln2_b, wg, wu, wd, ln3_g, ln3_b, *, tm):
    t, d = h2d.shape
    d_ff = wg.shape[1]
    const = lambda i: (0, 0)
    tile = lambda w: pl.BlockSpec((tm, w), lambda i: (i, 0))
    resident = lambda a: pl.BlockSpec(a.shape, const, pipeline_mode=pl.Buffered(1))
    return pl.pallas_call(
        functools.partial(_mix_ffn_ln_kernel, ff_chunk=MXU_DIM),
        grid=(t // tm,),
        in_specs=[tile(d), tile(oa2d.shape[1]), tile(ob2d.shape[1]), resident(w_a), resident(w_b),
                  pl.BlockSpec((1, d), const), pl.BlockSpec((1, d), const),
                  resident(wg), resident(wu), resident(wd),
                  pl.BlockSpec((1, d), const), pl.BlockSpec((1, d), const)],
        out_specs=tile(d),
        out_shape=jax.ShapeDtypeStruct((t, d), F32),
        scratch_shapes=[pltpu.VMEM((tm, d_ff), BF16)],
        compiler_params=pltpu.CompilerParams(
            dimension_semantics=("parallel",), vmem_limit_bytes=VMEM_LIMIT_BYTES),
        name="mix_ffn_ln",
    )(h2d, oa2d, ob2d, w_a, w_b, ln2_g, ln2_b, wg, wu, wd, ln3_g, ln3_b)


def _ffn_ln(x2d, wg, wu, wd, ln_g, ln_b, *, tm):
    t, d = x2d.shape
    d_ff = wg.shape[1]
    const = lambda i: (0, 0)
    return pl.pallas_call(
        functools.partial(_ffn_ln_kernel, ff_chunk=MXU_DIM),
        grid=(t // tm,),
        in_specs=[
            pl.BlockSpec((tm, d), lambda i: (i, 0)),
            pl.BlockSpec((d, d_ff), const, pipeline_mode=pl.Buffered(1)),
            pl.BlockSpec((d, d_ff), const, pipeline_mode=pl.Buffered(1)),
            pl.BlockSpec((d_ff, d), const, pipeline_mode=pl.Buffered(1)),
            pl.BlockSpec((1, d), const),
            pl.BlockSpec((1, d), const),
        ],
        out_specs=pl.BlockSpec((tm, d), lambda i: (i, 0)),
        out_shape=jax.ShapeDtypeStruct((t, d), F32),
        scratch_shapes=[pltpu.VMEM((tm, d_ff), BF16)],
        compiler_params=pltpu.CompilerParams(
            dimension_semantics=("parallel",), vmem_limit_bytes=VMEM_LIMIT_BYTES),
        name="ffn_ln",
    )(x2d, wg, wu, wd, ln_g, ln_b)


def _proj_kernel(h_ref, w_ref, o_ref):
    o_ref[...] = jnp.dot(h_ref[...].astype(BF16), w_ref[...], preferred_element_type=F32)


def _proj(h2d, w, *, tm):
    t, d = h2d.shape
    n = w.shape[1]
    return pl.pallas_call(
        _proj_kernel,
        grid=(t // tm,),
        in_specs=[
            pl.BlockSpec((tm, d), lambda i: (i, 0)),
            pl.BlockSpec((d, n), lambda i: (0, 0), pipeline_mode=pl.Buffered(1)),
        ],
        out_specs=pl.BlockSpec((tm, n), lambda i: (i, 0)),
        out_shape=jax.ShapeDtypeStruct((t, n), F32),
        compiler_params=pltpu.CompilerParams(
            dimension_semantics=("parallel",), vmem_limit_bytes=VMEM_LIMIT_BYTES),
        name="proj",
    )(h2d, w)


def _iota2(shape, dim):
    return lax.broadcasted_iota(jnp.int32, shape, dim)


def _gated_rmsnorm(o, gate, w):
    ms = jnp.mean(o * o, axis=-1, keepdims=True)
    return o * lax.rsqrt(ms + RMS_EPS) * w * _silu(gate)


GDN_ROWS_PER_STEP = 4

def _gdn_kernel(qkv_ref, z_ref, sm_ref, convw_ref, alog_ref, dtb_ref, nw_ref, tril_ref,
                o_ref, xpad_ref, act_ref, s_ref, wq_ref, u_ref, kd_ref, qk_ref, dl_ref, *, lb):
    c = CHUNK
    halo = 8
    n_chunks = lb // c
    n_rows = qkv_ref.shape[0]
    n_slabs = CONV_CH // LANES
    n_lanes = n_rows * GDN_HEADS
    lanes = [(r, h) for r in range(n_rows) for h in range(GDN_HEADS)]
    rows = lambda ci: slice(ci * c, (ci + 1) * c)
    nw = nw_ref[...]

    @pl.when(pl.program_id(1) == 0)
    def _():
        xpad_ref[:, 0:halo, :] = jnp.zeros((n_rows * n_slabs, halo, LANES), F32)
        for ref in (s_ref, wq_ref, u_ref, kd_ref, qk_ref, dl_ref):
            ref[...] = jnp.zeros(ref.shape, ref.dtype)

    state = [s_ref[ln] for ln in range(n_lanes)]
    for ci in range(n_chunks):
        idx = [ci * n_lanes + ln for ln in range(n_lanes)]
        ws = [_dot(wq_ref[i], state[ln]) for ln, i in enumerate(idx)]
        v_new = [u_ref[i] - ws[ln][:c] for ln, i in enumerate(idx)]
        upd = [_dot_tn(kd_ref[i], v_new[ln]) for ln, i in enumerate(idx)]
        intra = [_dot(qk_ref[i], v_new[ln]) for ln, i in enumerate(idx)]
        for ln, i in enumerate(idx):
            r, h = lanes[ln]
            state[ln] = state[ln] * dl_ref[i] + upd[ln]
            zg = z_ref[r, rows(ci), h * GDN_DV:(h + 1) * GDN_DV]
            o_ref[r, rows(ci), h * GDN_DV:(h + 1) * GDN_DV] = _gated_rmsnorm(ws[ln][c:] + intra[ln], zg, nw)
    for ln in range(n_lanes):
        s_ref[ln] = state[ln]

    pitch = (halo + lb) // 8
    assert pitch * 8 == halo + lb and pitch % 8 != 0

    def conv(r):
        for j in range(n_slabs):
            sj = r * n_slabs + j
            ls = slice(j * LANES, (j + 1) * LANES)
            xpad_ref[sj, halo:halo + lb, :] = qkv_ref[r, :, ls]
            taps = [jnp.broadcast_to(convw_ref[k:k + 1, ls], (8, LANES)) for k in range(CONV_K)]
            tiles = [xpad_ref[sj, pl.ds(g, 8, stride=pitch), :] for g in range(pitch)]
            wrapped = {-k: pltpu.roll(tiles[pitch - k], 1, 0) for k in range(1, CONV_K)}
            back = lambda g: tiles[g] if g >= 0 else wrapped[g]
            is_qk = j < 2 * GDN_HEADS
            scale = GDN_DK ** -0.5 if j < GDN_HEADS else 1.0
            for g in range(pitch):
                y = taps[CONV_K - 1] * tiles[g]
                for k in range(CONV_K - 1):
                    y = y + taps[k] * back(g - (CONV_K - 1 - k))
                y = _silu(y)
                if is_qk:
                    y = y * (lax.rsqrt(jnp.sum(y * y, axis=-1, keepdims=True) + RMS_EPS) * scale)
                act_ref[sj, pl.ds(g, 8, stride=pitch), :] = y
            xpad_ref[sj, 0:halo, :] = xpad_ref[sj, lb:lb + halo, :]

    row = _iota2((c, c), 0)
    col = _iota2((c, c), 1)
    causal = row >= col
    strict = row > col
    eye = (row == col).astype(F32)
    tril = tril_ref[...]

    beta_all, g_cum, g_cum_t = [], [], []
    for r in range(n_rows):
        sm = sm_ref[r]
        beta_all.append(_sigmoid(sm))
        g_all = -jnp.exp(alog_ref[...]) * _softplus(sm + dtb_ref[...])
        g_cum.append([_sum_dot(tril, g_all[rows(ci), :]) for ci in range(n_chunks)])
        g_cum_t.append([jnp.transpose(jnp.concatenate([g, g], axis=0))[:, :c] for g in g_cum[r]])

    def solve(chains):
        qs, ks, vs, betas, gcs, decays, e_gs, g_lasts = {}, {}, {}, {}, {}, {}, {}, {}
        for ci, ln in chains:
            r, h = lanes[ln]
            rs = rows(ci)
            ars = slice(halo + ci * c, halo + (ci + 1) * c)
            qs[ci, ln] = act_ref[r * n_slabs + h, ars, :]
            ks[ci, ln] = act_ref[r * n_slabs + GDN_HEADS + h, ars, :]
            vs[ci, ln] = act_ref[r * n_slabs + 2 * GDN_HEADS + h, ars, :]
            betas[ci, ln] = beta_all[r][rs, h:h + 1]
            gc = g_cum[r][ci][:, GDN_HEADS + h:GDN_HEADS + h + 1]
            gr = g_cum_t[r][ci][GDN_HEADS + h:GDN_HEADS + h + 1, :]
            gcs[ci, ln] = gc
            g_lasts[ci, ln] = gc[c - 1:c, :]
            decays[ci, ln] = jnp.where(causal, jnp.exp(jnp.where(causal, gc - gr, 0.0)), 0.0)
            e_gs[ci, ln] = jnp.exp(gc)

        qk_kk = {ch: _dot_nt(jnp.concatenate([qs[ch], ks[ch]], axis=0), ks[ch]) for ch in chains}
        qk = {ch: qk_kk[ch][:c] * decays[ch] for ch in chains}
        a_neg = {ch: jnp.where(strict, -(betas[ch] * qk_kk[ch][c:] * decays[ch]), 0.0) for ch in chains}
        a_bf = {ch: a_neg[ch].astype(BF16) for ch in chains}
        n_pow = {ch: _dot(a_bf[ch], a_bf[ch]).astype(BF16) for ch in chains}
        t_inv = {ch: eye + a_neg[ch] for ch in chains}
        for _ in range(4):
            y = {ch: _dot(jnp.concatenate([n_pow[ch], t_inv[ch].astype(BF16)], axis=0), n_pow[ch])
                 for ch in chains}
            n_pow = {ch: y[ch][:c].astype(BF16) for ch in chains}
            t_inv = {ch: t_inv[ch] + y[ch][c:] for ch in chains}
        corr = {ch: _dot(t_inv[ch], n_pow[ch]) for ch in chains}
        t_inv = {ch: t_inv[ch] + corr[ch] for ch in chains}

        sol = {ch: _dot(t_inv[ch], jnp.concatenate(
            [vs[ch] * betas[ch], ks[ch] * (betas[ch] * e_gs[ch])], axis=1)) for ch in chains}
        wq = {ch: jnp.concatenate([sol[ch][:, GDN_DV:], qs[ch] * e_gs[ch]], axis=0) for ch in chains}
        k_dec = {ch: ks[ch] * jnp.exp(g_lasts[ch] - gcs[ch]) for ch in chains}

        for ci, ln in chains:
            i = ci * n_lanes + ln
            wq_ref[i] = wq[ci, ln].astype(BF16)
            u_ref[i] = sol[ci, ln][:, :GDN_DV]
            kd_ref[i] = k_dec[ci, ln].astype(BF16)
            qk_ref[i] = qk[ci, ln].astype(BF16)
            dl_ref[i] = jnp.broadcast_to(jnp.exp(g_lasts[ci, ln]), (1, GDN_DV))

    for r in range(n_rows):
        conv(r)
        solve([(ci, r * GDN_HEADS + h) for ci in range(n_chunks) for h in range(GDN_HEADS)])


def _gdn(proj3d, conv_w, alog_row, dtb_row, norm_w, tril, *, lb):
    b, l, _ = proj3d.shape
    nb = l // lb
    nr = GDN_ROWS_PER_STEP if b % GDN_ROWS_PER_STEP == 0 else 1
    n_chains = (lb // CHUNK) * GDN_HEADS * nr
    n_slabs = CONV_CH // LANES
    const2 = lambda bi, ti: (0, 0)
    prep = lambda ti: jnp.minimum(ti, nb - 1)
    scan = lambda ti: jnp.maximum(ti - 1, 0)
    return pl.pallas_call(
        functools.partial(_gdn_kernel, lb=lb),
        grid=(b // nr, nb + 1),
        in_specs=[
            pl.BlockSpec((nr, lb, CONV_CH), lambda bi, ti: (bi, prep(ti), OFF_QKV // CONV_CH)),
            pl.BlockSpec((nr, lb, GDN_V_W), lambda bi, ti: (bi, scan(ti), OFF_Z // GDN_V_W)),
            pl.BlockSpec((nr, lb, SMALL_W), lambda bi, ti: (bi, prep(ti), OFF_SMALL // SMALL_W)),
            pl.BlockSpec((CONV_K, CONV_CH), const2),
            pl.BlockSpec((1, SMALL_W), const2),
            pl.BlockSpec((1, SMALL_W), const2),
            pl.BlockSpec((1, GDN_DV), const2),
            pl.BlockSpec((CHUNK, 3 * CHUNK), const2),
        ],
        out_specs=pl.BlockSpec((nr, lb, GDN_V_W), lambda bi, ti: (bi, scan(ti), 0)),
        out_shape=jax.ShapeDtypeStruct((b, l, GDN_V_W), F32),
        scratch_shapes=[
            pltpu.VMEM((nr * n_slabs, lb + 8, LANES), F32),
            pltpu.VMEM((nr * n_slabs, lb + 8, LANES), F32),
            pltpu.VMEM((nr * GDN_HEADS, GDN_DK, GDN_DV), F32),
            pltpu.VMEM((n_chains, 2 * CHUNK, GDN_DK), BF16),
            pltpu.VMEM((n_chains, CHUNK, GDN_DV), F32),
            pltpu.VMEM((n_chains, CHUNK, GDN_DK), BF16),
            pltpu.VMEM((n_chains, CHUNK, CHUNK), BF16),
            pltpu.VMEM((n_chains, 1, GDN_DV), F32),
        ],
        compiler_params=pltpu.CompilerParams(
            dimension_semantics=("parallel", "arbitrary"), vmem_limit_bytes=VMEM_LIMIT_BYTES),
        name="gdn",
    )(proj3d, proj3d, proj3d, conv_w, alog_row, dtb_row, norm_w, tril)


GLA_LEVELS = (32, 16, 8, 4, 2, 1)


def _gla_sum_matrix():
    c = CHUNK
    r = np.arange(c)[:, None]
    s = np.arange(c)[None, :]
    blocks = [(s <= r), (s > r)]
    for hs in GLA_LEVELS:
        pos = r % (2 * hs)
        m = r - pos + hs
        later = pos >= hs
        blocks.append(np.where(later, (s > m) & (s <= r), (s > r) & (s <= m)))
    return np.concatenate(blocks, axis=0).astype(np.float32)


def _gla_kernel(q_ref, k_ref, v_ref, g_ref, sm_ref, wgk_ref, bgk_ref, nw_ref, lmat_ref,
                o_ref, s_ref, *, lb):
    c = CHUNK

    @pl.when(pl.program_id(1) == 0)
    def _():
        s_ref[...] = jnp.zeros(s_ref.shape, F32)

    row = _iota2((c, c), 0)
    col = _iota2((c, c), 1)
    eye = row == col
    lmat = lmat_ref[...]
    nw = nw_ref[...]
    scale = GLA_DK ** -0.5

    gate_pre = _dot_split(sm_ref[0], wgk_ref[...]) + bgk_ref[...]
    logg_all = (jnp.minimum(gate_pre, 0.0) - jnp.log(1.0 + jnp.exp(-jnp.abs(gate_pre)))) / GLA_NORMALIZER

    n_chunks = lb // c
    chains = [(ci, h) for ci in range(n_chunks) for h in range(GLA_HEADS)]
    rows = lambda ci: slice(ci * c, (ci + 1) * c)
    lanes = lambda h: slice(h * GLA_DK, (h + 1) * GLA_DK)

    sums = [_sum_dot(lmat, logg_all[rows(ci), :]) for ci in range(n_chunks)]
    q_in, k_out, e_last, qm, km, qk_diag = [], [], [], [], [], []
    for ci in range(n_chunks):
        q = q_ref[0, rows(ci), :] * scale
        k = k_ref[0, rows(ci), :]
        e_b = jnp.exp(sums[ci][0:c])
        e_last.append(e_b[c - 1:c, :])
        q_in.append(q * e_b)
        k_out.append(k * jnp.exp(sums[ci][c:2 * c]))
        qm_c, km_c = [], []
        for li, hs in enumerate(GLA_LEVELS):
            e_l = jnp.exp(sums[ci][(2 + li) * c:(3 + li) * c])
            qm_c.append(q * e_l)
            km_c.append(k * e_l)
        qm.append(qm_c)
        km.append(km_c)
        qk_diag.append(q * k)

    att = {(ci, h): jnp.where(eye, jnp.sum(qk_diag[ci][:, lanes(h)], axis=-1, keepdims=True), 0.0)
           for ci, h in chains}
    for li, hs in enumerate(GLA_LEVELS):
        part = {(ci, h): _dot_nt(qm[ci][li][:, lanes(h)], km[ci][li][:, lanes(h)]) for ci, h in chains}
        keep = ((row ^ col) < 2 * hs) & ((row & hs) != 0) & ((col & hs) == 0)
        for ch in chains:
            att[ch] = att[ch] + jnp.where(keep, part[ch], 0.0)
    vs = {(ci, h): v_ref[0, rows(ci), h * GLA_DV:(h + 1) * GLA_DV] for ci, h in chains}
    intra = {ch: _dot(att[ch], vs[ch]) for ch in chains}
    upd = {(ci, h): _dot_tn(vs[ci, h], k_out[ci][:, lanes(h)]) for ci, h in chains}

    state = [s_ref[h] for h in range(GLA_HEADS)]
    for ci in range(n_chunks):
        inter = [_dot_nt(q_in[ci][:, lanes(h)], state[h]) for h in range(GLA_HEADS)]
        for h in range(GLA_HEADS):
            state[h] = state[h] * e_last[ci][:, lanes(h)] + upd[ci, h]
            gg = g_ref[0, rows(ci), h * GLA_DV:(h + 1) * GLA_DV]
            o_ref[0, rows(ci), h * GLA_DV:(h + 1) * GLA_DV] = _gated_rmsnorm(inter[h] + intra[ci, h], gg, nw)
    for h in range(GLA_HEADS):
        s_ref[h] = state[h]


def _gla(proj3d, wgk_pad, bgk_row, norm_w, lmat, *, lb):
    b, l, _ = proj3d.shape
    const2 = lambda bi, ti: (0, 0)
    return pl.pallas_call(
        functools.partial(_gla_kernel, lb=lb),
        grid=(b, l // lb),
        in_specs=[
            pl.BlockSpec((1, lb, GLA_QK_W), lambda bi, ti: (bi, ti, OFF_GQ // GLA_QK_W)),
            pl.BlockSpec((1, lb, GLA_QK_W), lambda bi, ti: (bi, ti, OFF_GK // GLA_QK_W)),
            pl.BlockSpec((1, lb, GLA_V_W), lambda bi, ti: (bi, ti, OFF_GV // GLA_V_W)),
            pl.BlockSpec((1, lb, GLA_V_W), lambda bi, ti: (bi, ti, OFF_GG // GLA_V_W)),
            pl.BlockSpec((1, lb, SMALL_W), lambda bi, ti: (bi, ti, OFF_SMALL // SMALL_W)),
            pl.BlockSpec((SMALL_W, GLA_QK_W), const2),
            pl.BlockSpec((1, GLA_QK_W), const2),
            pl.BlockSpec((1, GLA_DV), const2),
            pl.BlockSpec(lmat.shape, const2),
        ],
        out_specs=pl.BlockSpec((1, lb, GLA_V_W), lambda bi, ti: (bi, ti, 0)),
        out_shape=jax.ShapeDtypeStruct((b, l, GLA_V_W), F32),
        scratch_shapes=[pltpu.VMEM((GLA_HEADS, GLA_DV, GLA_DK), F32)],
        compiler_params=pltpu.CompilerParams(
            dimension_semantics=("parallel", "arbitrary"), vmem_limit_bytes=VMEM_LIMIT_BYTES),
        name="gla",
    )(proj3d, proj3d, proj3d, proj3d, proj3d, wgk_pad, bgk_row, norm_w, lmat)


def _regroup_w_in(w_in):
    d = w_in.shape[0]
    o = 0
    qkv = w_in[:, o:o + CONV_CH]; o += CONV_CH
    z = w_in[:, o:o + GDN_V_W]; o += GDN_V_W
    beta = w_in[:, o:o + GDN_HEADS]; o += GDN_HEADS
    dec = w_in[:, o:o + GDN_HEADS]; o += GDN_HEADS
    gq = w_in[:, o:o + GLA_QK_W]; o += GLA_QK_W
    gk = w_in[:, o:o + GLA_QK_W]; o += GLA_QK_W
    gv = w_in[:, o:o + GLA_V_W]; o += GLA_V_W
    gg = w_in[:, o:o + GLA_V_W]; o += GLA_V_W
    lr = w_in[:, o:o + GLA_RANK]
    small = jnp.concatenate(
        [beta, dec, lr, jnp.zeros((d, SMALL_W - 2 * GDN_HEADS - GLA_RANK), w_in.dtype)], axis=1)
    return jnp.concatenate([qkv, z, gq, gk, gv, gg, small], axis=1)


def _lane_row(vals, offset, width):
    row = jnp.zeros((1, width), F32)
    return row.at[0, offset:offset + vals.shape[0]].set(vals.astype(F32))


def _layer(x, p, i, *, tm, lb_gdn, lb_gla):
    bsz, l, d = x.shape
    t = bsz * l
    row = lambda a: a.reshape(1, -1).astype(F32)

    h = _ffn_ln(x.reshape(t, d), p["ffn1_w_gate"][i].astype(BF16), p["ffn1_w_up"][i].astype(BF16),
                p["ffn1_w_down"][i].astype(BF16), row(p["ln1_g"][i]), row(p["ln1_b"][i]), tm=tm)

    proj = _proj(h, _regroup_w_in(p["w_in"][i].astype(BF16)), tm=tm).reshape(bsz, l, PROJ_W)

    tril = jnp.asarray(np.tile(np.tril(np.ones((CHUNK, CHUNK), np.float32)), (1, 3)), BF16)
    o_a = _gdn(proj, p["conv_w"][i].astype(F32),
               _lane_row(p["a_log"][i], GDN_HEADS, SMALL_W), _lane_row(p["dt_bias"][i], GDN_HEADS, SMALL_W),
               row(p["gdn_norm_w"][i]), tril, lb=lb_gdn)

    wgk_pad = jnp.zeros((SMALL_W, GLA_QK_W), F32).at[2 * GDN_HEADS:2 * GDN_HEADS + GLA_RANK, :].set(
        p["w_gk"][i].astype(F32))
    o_b = _gla(proj, wgk_pad, row(p["b_gk"][i]), row(p["gla_norm_w"][i]),
               jnp.asarray(np.tile(_gla_sum_matrix(), (1, 3)), BF16), lb=lb_gla)

    w_out = p["w_out"][i].astype(BF16)
    out = _mix_ffn_ln(h, o_a.reshape(t, GDN_V_W), o_b.reshape(t, GLA_V_W), w_out[:GDN_V_W], w_out[GDN_V_W:],
                      row(p["ln2_g"][i]), row(p["ln2_b"][i]),
                      p["ffn2_w_gate"][i].astype(BF16), p["ffn2_w_up"][i].astype(BF16),
                      p["ffn2_w_down"][i].astype(BF16), row(p["ln3_g"][i]), row(p["ln3_b"][i]), tm=tm)
    return out.reshape(bsz, l, d)


def kernel(x, ffn1_w_gate, ffn1_w_up, ffn1_w_down, ln1_g, ln1_b, w_in, conv_w, a_log, dt_bias, gdn_norm_w,
           w_gk, b_gk, gla_norm_w, w_out, ln2_g, ln2_b, ffn2_w_gate, ffn2_w_up, ffn2_w_down, ln3_g, ln3_b):
    p = dict(ffn1_w_gate=ffn1_w_gate, ffn1_w_up=ffn1_w_up, ffn1_w_down=ffn1_w_down, ln1_g=ln1_g, ln1_b=ln1_b,
             w_in=w_in, conv_w=conv_w, a_log=a_log, dt_bias=dt_bias, gdn_norm_w=gdn_norm_w,
             w_gk=w_gk, b_gk=b_gk, gla_norm_w=gla_norm_w, w_out=w_out, ln2_g=ln2_g, ln2_b=ln2_b,
             ffn2_w_gate=ffn2_w_gate, ffn2_w_up=ffn2_w_up, ffn2_w_down=ffn2_w_down, ln3_g=ln3_g, ln3_b=ln3_b)
    bsz, l, _ = x.shape
    tm = min(1024, bsz * l)
    for i in range(ffn1_w_gate.shape[0]):
        x = _layer(x, p, i, tm=tm, lb_gdn=min(256, l), lb_gla=min(1024, l))
    return x
```

```python
import functools

import numpy as np
import jax
import jax.numpy as jnp
from jax import lax
from jax.experimental import pallas as pl
from jax.experimental.pallas import tpu as pltpu

GDN_HEADS = 4
GDN_DK = 128
GDN_DV = 128
CONV_K = 4
GLA_HEADS = 4
GLA_DK = 64
GLA_DV = 128
GLA_RANK = 16
GLA_NORMALIZER = 16.0
CHUNK = 64
DEPTH = 1
ALPHA = (2.0 * DEPTH) ** 0.25
LN_EPS = 1e-5
RMS_EPS = 1e-6

GDN_QK_W = GDN_HEADS * GDN_DK
GDN_V_W = GDN_HEADS * GDN_DV
GLA_QK_W = GLA_HEADS * GLA_DK
GLA_V_W = GLA_HEADS * GLA_DV
CONV_CH = 2 * GDN_QK_W + GDN_V_W

LANES = 128
SUBLANES = 8
MXU_DIM = 256
VMEM_LIMIT_BYTES = 56 * 1024 * 1024

FFN_TOKENS = 1024
GDN_BLOCK = 256
GLA_BLOCK = 1024

SMALL_W = LANES
PROJ_W = CONV_CH + GDN_V_W + 2 * GLA_QK_W + 2 * GLA_V_W + SMALL_W
OFF_QKV = 0
OFF_Z = CONV_CH
OFF_GQ = OFF_Z + GDN_V_W
OFF_GK = OFF_GQ + GLA_QK_W
OFF_GV = OFF_GK + GLA_QK_W
OFF_GG = OFF_GV + GLA_V_W
OFF_SMALL = OFF_GG + GLA_V_W

BF16 = jnp.bfloat16
F32 = jnp.float32


def _dot(a, b):
    return jnp.dot(a.astype(BF16), b.astype(BF16), preferred_element_type=F32)


def _dot_nt(a, b):
    return lax.dot_general(a.astype(BF16), b.astype(BF16), (((1,), (1,)), ((), ())),
                           preferred_element_type=F32)


def _dot_tn(a, b):
    return lax.dot_general(a.astype(BF16), b.astype(BF16), (((0,), (0,)), ((), ())),
                           preferred_element_type=F32)


def _split_bf16(x, pieces):
    out = []
    for _ in range(pieces - 1):
        p = x.astype(BF16)
        out.append(p)
        x = x - p.astype(F32)
    out.append(x.astype(BF16))
    return out


def _sum_dot(sel3, x):
    return jnp.dot(sel3, jnp.concatenate(_split_bf16(x, 3), axis=0), preferred_element_type=F32)


def _dot_split(a, b):
    a_hi, a_lo = _split_bf16(a, 2)
    b_hi, b_lo = _split_bf16(b, 2)
    return jnp.dot(jnp.concatenate([a_hi, a_lo, a_hi], axis=1), jnp.concatenate([b_hi, b_hi, b_lo], axis=0),
                   preferred_element_type=F32)


NEG_LOG2_E = -1.4426950408889634


def _sigmoid(x):
    return 1.0 / (1.0 + jnp.exp2(x * NEG_LOG2_E))


def _silu(x):
    return x * _sigmoid(x)


def _softplus(x):
    return jnp.maximum(x, 0.0) + jnp.log(1.0 + jnp.exp(-jnp.abs(x)))


def _layer_norm_rows(r, g, b):
    mu = jnp.mean(r, axis=-1, keepdims=True)
    c = r - mu
    var = jnp.mean(c * c, axis=-1, keepdims=True)
    return c * lax.rsqrt(var + LN_EPS) * g + b


ROW_SPLITS = 4


def _row_groups(tm):
    step = tm // ROW_SPLITS
    return [slice(r * step, (r + 1) * step) for r in range(ROW_SPLITS)]


def _swiglu_ln(x, wg_ref, wu_ref, wd_ref, g_ref, b_ref, act_ref, o_ref, ff_chunk):
    xb = x.astype(BF16)
    d_ff = wg_ref.shape[1]
    for j in range(d_ff // ff_chunk):
        sl = slice(j * ff_chunk, (j + 1) * ff_chunk)
        gate = jnp.dot(xb, wg_ref[:, sl], preferred_element_type=F32)
        up = jnp.dot(xb, wu_ref[:, sl], preferred_element_type=F32)
        act_ref[:, sl] = (_silu(gate) * up).astype(BF16)
    for rs in _row_groups(x.shape[0]):
        y = jnp.dot(act_ref[rs, :], wd_ref[...], preferred_element_type=F32)
        o_ref[rs, :] = _layer_norm_rows(ALPHA * x[rs, :] + 0.5 * y, g_ref[...], b_ref[...])


def _ffn_ln_kernel(x_ref, wg_ref, wu_ref, wd_ref, g_ref, b_ref, o_ref, act_ref, *, ff_chunk):
    _swiglu_ln(x_ref[...], wg_ref, wu_ref, wd_ref, g_ref, b_ref, act_ref, o_ref, ff_chunk)


def _mix_ffn_ln_kernel(h_ref, oa_ref, ob_ref, wa_ref, wb_ref, g2_ref, b2_ref,
                       wg_ref, wu_ref, wd_ref, g3_ref, b3_ref, o_ref, act_ref, *, ff_chunk):
    xs = []
    for rs in _row_groups(h_ref.shape[0]):
        mix = (jnp.dot(oa_ref[rs, :].astype(BF16), wa_ref[...], preferred_element_type=F32)
               + jnp.dot(ob_ref[rs, :].astype(BF16), wb_ref[...], preferred_element_type=F32))
        xs.append(_layer_norm_rows(ALPHA * h_ref[rs, :] + mix, g2_ref[...], b2_ref[...]))
    _swiglu_ln(jnp.concatenate(xs, axis=0), wg_ref, wu_ref, wd_ref, g3_ref, b3_ref, act_ref, o_ref, ff_chunk)


def _mix_ffn_ln(h2d, oa2d, ob2d, w_a, w_b, ln2_g, ln2_b, wg, wu, wd, ln3_g, ln3_b, *, tm):
    t, d = h2d.shape
    d_ff = wg.shape[1]
    const = lambda i: (0, 0)
    tile = lambda w: pl.BlockSpec((tm, w), lambda i: (i, 0))
    resident = lambda a: pl.BlockSpec(a.shape, const, pipeline_mode=pl.Buffered(1))
    return pl.pallas_call(
        functools.partial(_mix_ffn_ln_kernel, ff_chunk=MXU_DIM),
        grid=(t // tm,),
        in_specs=[tile(d), tile(oa2d.shape[1]), tile(ob2d.shape[1]), resident(w_a), resident(w_b),
                  pl.BlockSpec((1, d), const), pl.BlockSpec((1, d), const),
                  resident(wg), resident(wu), resident(wd),
                  pl.BlockSpec((1, d), const), pl.BlockSpec((1, d), const)],
        out_specs=tile(d),
        out_shape=jax.ShapeDtypeStruct((t, d), F32),
        scratch_shapes=[pltpu.VMEM((tm, d_ff), BF16)],
        compiler_params=pltpu.CompilerParams(
            dimension_semantics=("parallel",), vmem_limit_bytes=VMEM_LIMIT_BYTES),
        name="mix_ffn_ln",
    )(h2d, oa2d, ob2d, w_a, w_b, ln2_g, ln2_b, wg, wu, wd, ln3_g, ln3_b)


def _ffn_ln(x2d, wg, wu, wd, ln_g, ln_b, *, tm):
    t, d = x2d.shape
    d_ff = wg.shape[1]
    const = lambda i: (0, 0)
    return pl.pallas_call(
        functools.partial(_ffn_ln_kernel, ff_chunk=MXU_DIM),
        grid=(t // tm,),
        in_specs=[
            pl.BlockSpec((tm, d), lambda i: (i, 0)),
            pl.BlockSpec((d, d_ff), const, pipeline_mode=pl.Buffered(1)),
            pl.BlockSpec((d, d_ff), const, pipeline_mode=pl.Buffered(1)),
            pl.BlockSpec((d_ff, d), const, pipeline_mode=pl.Buffered(1)),
            pl.BlockSpec((1, d), const),
            pl.BlockSpec((1, d), const),
        ],
        out_specs=pl.BlockSpec((tm, d), lambda i: (i, 0)),
        out_shape=jax.ShapeDtypeStruct((t, d), F32),
        scratch_shapes=[pltpu.VMEM((tm, d_ff), BF16)],
        compiler_params=pltpu.CompilerParams(
            dimension_semantics=("parallel",), vmem_limit_bytes=VMEM_LIMIT_BYTES),
        name="ffn_ln",
    )(x2d, wg, wu, wd, ln_g, ln_b)


def _proj_kernel(h_ref, w_ref, o_ref):
    o_ref[...] = jnp.dot(h_ref[...].astype(BF16), w_ref[...], preferred_element_type=F32)


def _proj(h2d, w, *, tm):
    t, d = h2d.shape
    n = w.shape[1]
    return pl.pallas_call(
        _proj_kernel,
        grid=(t // tm,),
        in_specs=[
            pl.BlockSpec((tm, d), lambda i: (i, 0)),
            pl.BlockSpec((d, n), lambda i: (0, 0), pipeline_mode=pl.Buffered(1)),
        ],
        out_specs=pl.BlockSpec((tm, n), lambda i: (i, 0)),
        out_shape=jax.ShapeDtypeStruct((t, n), F32),
        compiler_params=pltpu.CompilerParams(
            dimension_semantics=("parallel",), vmem_limit_bytes=VMEM_LIMIT_BYTES),
        name="proj",
    )(h2d, w)


def _iota2(shape, dim):
    return lax.broadcasted_iota(jnp.int32, shape, dim)


def _gated_rmsnorm(o, gate, w):
    ms = jnp.mean(o * o, axis=-1, keepdims=True)
    return o * lax.rsqrt(ms + RMS_EPS) * w * _silu(gate)


GDN_ROWS_PER_STEP = 4


def _gdn_kernel(qkv_ref, z_ref, sm_ref, convw_ref, alog_ref, dtb_ref, nw_ref, tril_ref,
                o_ref, xpad_ref, act_ref, s_ref, wq_ref, u_ref, kd_ref, qk_ref, dl_ref, *, lb):
    c = CHUNK
    halo = SUBLANES
    n_chunks = lb // c
    n_rows = qkv_ref.shape[0]
    n_slabs = CONV_CH // LANES
    n_lanes = n_rows * GDN_HEADS
    lanes = [(r, h) for r in range(n_rows) for h in range(GDN_HEADS)]
    chains = [(ci, ln) for ci in range(n_chunks) for ln in range(n_lanes)]
    rows = lambda ci: slice(ci * c, (ci + 1) * c)
    nw = nw_ref[...]

    @pl.when(pl.program_id(1) == 0)
    def _():
        xpad_ref[:, 0:halo, :] = jnp.zeros((n_rows * n_slabs, halo, LANES), F32)
        for ref in (s_ref, wq_ref, u_ref, kd_ref, qk_ref, dl_ref):
            ref[...] = jnp.zeros(ref.shape, ref.dtype)

    state = [s_ref[ln] for ln in range(n_lanes)]
    for ci in range(n_chunks):
        idx = [ci * n_lanes + ln for ln in range(n_lanes)]
        ws = [_dot(wq_ref[i], state[ln]) for ln, i in enumerate(idx)]
        v_new = [u_ref[i] - ws[ln][:c] for ln, i in enumerate(idx)]
        upd = [_dot_tn(kd_ref[i], v_new[ln]) for ln, i in enumerate(idx)]
        intra = [_dot(qk_ref[i], v_new[ln]) for ln, i in enumerate(idx)]
        for ln, i in enumerate(idx):
            r, h = lanes[ln]
            state[ln] = state[ln] * dl_ref[i] + upd[ln]
            zg = z_ref[r, rows(ci), h * GDN_DV:(h + 1) * GDN_DV]
            o_ref[r, rows(ci), h * GDN_DV:(h + 1) * GDN_DV] = _gated_rmsnorm(ws[ln][c:] + intra[ln], zg, nw)
    for ln in range(n_lanes):
        s_ref[ln] = state[ln]

    pitch = (halo + lb) // SUBLANES
    assert pitch * SUBLANES == halo + lb and pitch % 8 != 0
    for r in range(n_rows):
        for j in range(n_slabs):
            sj = r * n_slabs + j
            ls = slice(j * LANES, (j + 1) * LANES)
            xpad_ref[sj, halo:halo + lb, :] = qkv_ref[r, :, ls]
            taps = [jnp.broadcast_to(convw_ref[k:k + 1, ls], (SUBLANES, LANES)) for k in range(CONV_K)]
            tiles = [xpad_ref[sj, pl.ds(g, SUBLANES, stride=pitch), :] for g in range(pitch)]
            wrapped = {-k: pltpu.roll(tiles[pitch - k], 1, 0) for k in range(1, CONV_K)}
            back = lambda g: tiles[g] if g >= 0 else wrapped[g]
            is_qk = j < 2 * GDN_HEADS
            scale = GDN_DK ** -0.5 if j < GDN_HEADS else 1.0
            for g in range(pitch):
                y = taps[CONV_K - 1] * tiles[g]
                for k in range(CONV_K - 1):
                    y = y + taps[k] * back(g - (CONV_K - 1 - k))
                y = _silu(y)
                if is_qk:
                    y = y * (lax.rsqrt(jnp.sum(y * y, axis=-1, keepdims=True) + RMS_EPS) * scale)
                act_ref[sj, pl.ds(g, SUBLANES, stride=pitch), :] = y
            xpad_ref[sj, 0:halo, :] = xpad_ref[sj, lb:lb + halo, :]

    row = _iota2((c, c), 0)
    col = _iota2((c, c), 1)
    causal = row >= col
    strict = row > col
    eye = (row == col).astype(F32)
    tril = tril_ref[...]

    beta_all, g_cum, g_cum_t = [], [], []
    for r in range(n_rows):
        sm = sm_ref[r]
        beta_all.append(_sigmoid(sm))
        g_all = -jnp.exp(alog_ref[...]) * _softplus(sm + dtb_ref[...])
        g_cum.append([_sum_dot(tril, g_all[rows(ci), :]) for ci in range(n_chunks)])
        g_cum_t.append([jnp.transpose(jnp.concatenate([g, g], axis=0))[:, :c] for g in g_cum[r]])

    qs, ks, vs, betas, gcs, decays, e_gs, g_lasts = {}, {}, {}, {}, {}, {}, {}, {}
    for ci, ln in chains:
        r, h = lanes[ln]
        rs = rows(ci)
        ars = slice(halo + ci * c, halo + (ci + 1) * c)
        qs[ci, ln] = act_ref[r * n_slabs + h, ars, :]
        ks[ci, ln] = act_ref[r * n_slabs + GDN_HEADS + h, ars, :]
        vs[ci, ln] = act_ref[r * n_slabs + 2 * GDN_HEADS + h, ars, :]
        betas[ci, ln] = beta_all[r][rs, h:h + 1]
        gc = g_cum[r][ci][:, GDN_HEADS + h:GDN_HEADS + h + 1]
        gr = g_cum_t[r][ci][GDN_HEADS + h:GDN_HEADS + h + 1, :]
        gcs[ci, ln] = gc
        g_lasts[ci, ln] = gc[c - 1:c, :]
        decays[ci, ln] = jnp.where(causal, jnp.exp(jnp.where(causal, gc - gr, 0.0)), 0.0)
        e_gs[ci, ln] = jnp.exp(gc)

    qk_kk = {ch: _dot_nt(jnp.concatenate([qs[ch], ks[ch]], axis=0), ks[ch]) for ch in chains}
    qk = {ch: qk_kk[ch][:c] * decays[ch] for ch in chains}
    a_neg = {ch: jnp.where(strict, -(betas[ch] * qk_kk[ch][c:] * decays[ch]), 0.0) for ch in chains}
    a_bf = {ch: a_neg[ch].astype(BF16) for ch in chains}
    n_pow = {ch: _dot(a_bf[ch], a_bf[ch]).astype(BF16) for ch in chains}
    t_inv = {ch: eye + a_neg[ch] for ch in chains}
    for _ in range(4):
        y = {ch: _dot(jnp.concatenate([n_pow[ch], t_inv[ch].astype(BF16)], axis=0), n_pow[ch]) for ch in chains}
        n_pow = {ch: y[ch][:c].astype(BF16) for ch in chains}
        t_inv = {ch: t_inv[ch] + y[ch][c:] for ch in chains}
    corr = {ch: _dot(t_inv[ch], n_pow[ch]) for ch in chains}
    t_inv = {ch: t_inv[ch] + corr[ch] for ch in chains}

    sol = {ch: _dot(t_inv[ch], jnp.concatenate(
        [vs[ch] * betas[ch], ks[ch] * (betas[ch] * e_gs[ch])], axis=1)) for ch in chains}
    wq = {ch: jnp.concatenate([sol[ch][:, GDN_DV:], qs[ch] * e_gs[ch]], axis=0) for ch in chains}
    k_dec = {ch: ks[ch] * jnp.exp(g_lasts[ch] - gcs[ch]) for ch in chains}

    for ci, ln in chains:
        i = ci * n_lanes + ln
        wq_ref[i] = wq[ci, ln].astype(BF16)
        u_ref[i] = sol[ci, ln][:, :GDN_DV]
        kd_ref[i] = k_dec[ci, ln].astype(BF16)
        qk_ref[i] = qk[ci, ln].astype(BF16)
        dl_ref[i] = jnp.broadcast_to(jnp.exp(g_lasts[ci, ln]), (1, GDN_DV))


def _gdn(proj3d, conv_w, alog_row, dtb_row, norm_w, tril, *, lb):
    b, l, _ = proj3d.shape
    nb = l // lb
    nr = GDN_ROWS_PER_STEP if b % GDN_ROWS_PER_STEP == 0 else 1
    n_chains = (lb // CHUNK) * GDN_HEADS * nr
    n_slabs = CONV_CH // LANES
    const2 = lambda bi, ti: (0, 0)
    prep = lambda ti: jnp.minimum(ti, nb - 1)
    scan = lambda ti: jnp.maximum(ti - 1, 0)
    return pl.pallas_call(
        functools.partial(_gdn_kernel, lb=lb),
        grid=(b // nr, nb + 1),
        in_specs=[
            pl.BlockSpec((nr, lb, CONV_CH), lambda bi, ti: (bi, prep(ti), OFF_QKV // CONV_CH)),
            pl.BlockSpec((nr, lb, GDN_V_W), lambda bi, ti: (bi, scan(ti), OFF_Z // GDN_V_W)),
            pl.BlockSpec((nr, lb, SMALL_W), lambda bi, ti: (bi, prep(ti), OFF_SMALL // SMALL_W)),
            pl.BlockSpec((CONV_K, CONV_CH), const2),
            pl.BlockSpec((1, SMALL_W), const2),
            pl.BlockSpec((1, SMALL_W), const2),
            pl.BlockSpec((1, GDN_DV), const2),
            pl.BlockSpec((CHUNK, 3 * CHUNK), const2),
        ],
        out_specs=pl.BlockSpec((nr, lb, GDN_V_W), lambda bi, ti: (bi, scan(ti), 0)),
        out_shape=jax.ShapeDtypeStruct((b, l, GDN_V_W), F32),
        scratch_shapes=[
            pltpu.VMEM((nr * n_slabs, lb + SUBLANES, LANES), F32),
            pltpu.VMEM((nr * n_slabs, lb + SUBLANES, LANES), F32),
            pltpu.VMEM((nr * GDN_HEADS, GDN_DK, GDN_DV), F32),
            pltpu.VMEM((n_chains, 2 * CHUNK, GDN_DK), BF16),
            pltpu.VMEM((n_chains, CHUNK, GDN_DV), F32),
            pltpu.VMEM((n_chains, CHUNK, GDN_DK), BF16),
            pltpu.VMEM((n_chains, CHUNK, CHUNK), BF16),
            pltpu.VMEM((n_chains, 1, GDN_DV), F32),
        ],
        compiler_params=pltpu.CompilerParams(
            dimension_semantics=("parallel", "arbitrary"), vmem_limit_bytes=VMEM_LIMIT_BYTES),
        name="gdn",
    )(proj3d, proj3d, proj3d, conv_w, alog_row, dtb_row, norm_w, tril)


GLA_LEVELS = (32, 16, 8, 4, 2, 1)


def _gla_sum_matrix():
    c = CHUNK
    r = np.arange(c)[:, None]
    s = np.arange(c)[None, :]
    blocks = [(s <= r), (s > r)]
    for hs in GLA_LEVELS:
        pos = r % (2 * hs)
        m = r - pos + hs
        later = pos >= hs
        blocks.append(np.where(later, (s > m) & (s <= r), (s > r) & (s <= m)))
    return np.concatenate(blocks, axis=0).astype(np.float32)


def _gla_kernel(q_ref, k_ref, v_ref, g_ref, sm_ref, wgk_ref, bgk_ref, nw_ref, lmat_ref,
                o_ref, s_ref, *, lb):
    c = CHUNK

    @pl.when(pl.program_id(1) == 0)
    def _():
        s_ref[...] = jnp.zeros(s_ref.shape, F32)

    row = _iota2((c, c), 0)
    col = _iota2((c, c), 1)
    eye = row == col
    lmat = lmat_ref[...]
    nw = nw_ref[...]
    scale = GLA_DK ** -0.5

    gate_pre = _dot_split(sm_ref[0], wgk_ref[...]) + bgk_ref[...]
    logg_all = (jnp.minimum(gate_pre, 0.0) - jnp.log(1.0 + jnp.exp(-jnp.abs(gate_pre)))) / GLA_NORMALIZER

    n_chunks = lb // c
    chains = [(ci, h) for ci in range(n_chunks) for h in range(GLA_HEADS)]
    rows = lambda ci: slice(ci * c, (ci + 1) * c)
    lanes = lambda h: slice(h * GLA_DK, (h + 1) * GLA_DK)

    sums = [_sum_dot(lmat, logg_all[rows(ci), :]) for ci in range(n_chunks)]
    q_in, k_out, e_last, qm, km, qk_diag = [], [], [], [], [], []
    for ci in range(n_chunks):
        q = q_ref[0, rows(ci), :] * scale
        k = k_ref[0, rows(ci), :]
        e_b = jnp.exp(sums[ci][0:c])
        e_last.append(e_b[c - 1:c, :])
        q_in.append(q * e_b)
        k_out.append(k * jnp.exp(sums[ci][c:2 * c]))
        qm_c, km_c = [], []
        for li, hs in enumerate(GLA_LEVELS):
            e_l = jnp.exp(sums[ci][(2 + li) * c:(3 + li) * c])
            qm_c.append(q * e_l)
            km_c.append(k * e_l)
        qm.append(qm_c)
        km.append(km_c)
        qk_diag.append(q * k)

    att = {(ci, h): jnp.where(eye, jnp.sum(qk_diag[ci][:, lanes(h)], axis=-1, keepdims=True), 0.0)
           for ci, h in chains}
    for li, hs in enumerate(GLA_LEVELS):
        part = {(ci, h): _dot_nt(qm[ci][li][:, lanes(h)], km[ci][li][:, lanes(h)]) for ci, h in chains}
        keep = ((row ^ col) < 2 * hs) & ((row & hs) != 0) & ((col & hs) == 0)
        for ch in chains:
            att[ch] = att[ch] + jnp.where(keep, part[ch], 0.0)
    vs = {(ci, h): v_ref[0, rows(ci), h * GLA_DV:(h + 1) * GLA_DV] for ci, h in chains}
    intra = {ch: _dot(att[ch], vs[ch]) for ch in chains}
    upd = {(ci, h): _dot_tn(vs[ci, h], k_out[ci][:, lanes(h)]) for ci, h in chains}

    state = [s_ref[h] for h in range(GLA_HEADS)]
    for ci in range(n_chunks):
        inter = [_dot_nt(q_in[ci][:, lanes(h)], state[h]) for h in range(GLA_HEADS)]
        for h in range(GLA_HEADS):
            state[h] = state[h] * e_last[ci][:, lanes(h)] + upd[ci, h]
            gg = g_ref[0, rows(ci), h * GLA_DV:(h + 1) * GLA_DV]
            o_ref[0, rows(ci), h * GLA_DV:(h + 1) * GLA_DV] = _gated_rmsnorm(inter[h] + intra[ci, h], gg, nw)
    for h in range(GLA_HEADS):
        s_ref[h] = state[h]


def _gla(proj3d, wgk_pad, bgk_row, norm_w, lmat, *, lb):
    b, l, _ = proj3d.shape
    const2 = lambda bi, ti: (0, 0)
    return pl.pallas_call(
        functools.partial(_gla_kernel, lb=lb),
        grid=(b, l // lb),
        in_specs=[
            pl.BlockSpec((1, lb, GLA_QK_W), lambda bi, ti: (bi, ti, OFF_GQ // GLA_QK_W)),
            pl.BlockSpec((1, lb, GLA_QK_W), lambda bi, ti: (bi, ti, OFF_GK // GLA_QK_W)),
            pl.BlockSpec((1, lb, GLA_V_W), lambda bi, ti: (bi, ti, OFF_GV // GLA_V_W)),
            pl.BlockSpec((1, lb, GLA_V_W), lambda bi, ti: (bi, ti, OFF_GG // GLA_V_W)),
            pl.BlockSpec((1, lb, SMALL_W), lambda bi, ti: (bi, ti, OFF_SMALL // SMALL_W)),
            pl.BlockSpec((SMALL_W, GLA_QK_W), const2),
            pl.BlockSpec((1, GLA_QK_W), const2),
            pl.BlockSpec((1, GLA_DV), const2),
            pl.BlockSpec(lmat.shape, const2),
        ],
        out_specs=pl.BlockSpec((1, lb, GLA_V_W), lambda bi, ti: (bi, ti, 0)),
        out_shape=jax.ShapeDtypeStruct((b, l, GLA_V_W), F32),
        scratch_shapes=[pltpu.VMEM((GLA_HEADS, GLA_DV, GLA_DK), F32)],
        compiler_params=pltpu.CompilerParams(
            dimension_semantics=("parallel", "arbitrary"), vmem_limit_bytes=VMEM_LIMIT_BYTES),
        name="gla",
    )(proj3d, proj3d, proj3d, proj3d, proj3d, wgk_pad, bgk_row, norm_w, lmat)


def _regroup_w_in(w_in):
    d = w_in.shape[0]
    o = 0
    qkv = w_in[:, o:o + CONV_CH]; o += CONV_CH
    z = w_in[:, o:o + GDN_V_W]; o += GDN_V_W
    beta = w_in[:, o:o + GDN_HEADS]; o += GDN_HEADS
    dec = w_in[:, o:o + GDN_HEADS]; o += GDN_HEADS
    gq = w_in[:, o:o + GLA_QK_W]; o += GLA_QK_W
    gk = w_in[:, o:o + GLA_QK_W]; o += GLA_QK_W
    gv = w_in[:, o:o + GLA_V_W]; o += GLA_V_W
    gg = w_in[:, o:o + GLA_V_W]; o += GLA_V_W
    lr = w_in[:, o:o + GLA_RANK]
    small = jnp.concatenate(
        [beta, dec, lr, jnp.zeros((d, SMALL_W - 2 * GDN_HEADS - GLA_RANK), w_in.dtype)], axis=1)
    return jnp.concatenate([qkv, z, gq, gk, gv, gg, small], axis=1)


def _lane_row(vals, offset, width):
    row = jnp.zeros((1, width), F32)
    return row.at[0, offset:offset + vals.shape[0]].set(vals.astype(F32))


def _layer(x, p, i, *, tm, lb_gdn, lb_gla):
    bsz, l, d = x.shape
    t = bsz * l
    row = lambda a: a.reshape(1, -1).astype(F32)

    h = _ffn_ln(x.reshape(t, d), p["ffn1_w_gate"][i].astype(BF16), p["ffn1_w_up"][i].astype(BF16),
                p["ffn1_w_down"][i].astype(BF16), row(p["ln1_g"][i]), row(p["ln1_b"][i]), tm=tm)

    proj = _proj(h, _regroup_w_in(p["w_in"][i].astype(BF16)), tm=tm).reshape(bsz, l, PROJ_W)

    tril = jnp.asarray(np.tile(np.tril(np.ones((CHUNK, CHUNK), np.float32)), (1, 3)), BF16)
    o_a = _gdn(proj, p["conv_w"][i].astype(F32),
               _lane_row(p["a_log"][i], GDN_HEADS, SMALL_W), _lane_row(p["dt_bias"][i], GDN_HEADS, SMALL_W),
               row(p["gdn_norm_w"][i]), tril, lb=lb_gdn)

    wgk_pad = jnp.zeros((SMALL_W, GLA_QK_W), F32).at[2 * GDN_HEADS:2 * GDN_HEADS + GLA_RANK, :].set(
        p["w_gk"][i].astype(F32))
    o_b = _gla(proj, wgk_pad, row(p["b_gk"][i]), row(p["gla_norm_w"][i]),
               jnp.asarray(np.tile(_gla_sum_matrix(), (1, 3)), BF16), lb=lb_gla)

    w_out = p["w_out"][i].astype(BF16)
    out = _mix_ffn_ln(h, o_a.reshape(t, GDN_V_W), o_b.reshape(t, GLA_V_W), w_out[:GDN_V_W], w_out[GDN_V_W:],
                      row(p["ln2_g"][i]), row(p["ln2_b"][i]),
                      p["ffn2_w_gate"][i].astype(BF16), p["ffn2_w_up"][i].astype(BF16),
                      p["ffn2_w_down"][i].astype(BF16), row(p["ln3_g"][i]), row(p["ln3_b"][i]), tm=tm)
    return out.reshape(bsz, l, d)


def kernel(x, ffn1_w_gate, ffn1_w_up, ffn1_w_down, ln1_g, ln1_b, w_in, conv_w, a_log, dt_bias, gdn_norm_w,
           w_gk, b_gk, gla_norm_w, w_out, ln2_g, ln2_b, ffn2_w_gate, ffn2_w_up, ffn2_w_down, ln3_g, ln3_b):
    p = dict(ffn1_w_gate=ffn1_w_gate, ffn1_w_up=ffn1_w_up, ffn1_w_down=ffn1_w_down, ln1_g=ln1_g, ln1_b=ln1_b,
             w_in=w_in, conv_w=conv_w, a_log=a_log, dt_bias=dt_bias, gdn_norm_w=gdn_norm_w,
             w_gk=w_gk, b_gk=b_gk, gla_norm_w=gla_norm_w, w_out=w_out, ln2_g=ln2_g, ln2_b=ln2_b,
             ffn2_w_gate=ffn2_w_gate, ffn2_w_up=ffn2_w_up, ffn2_w_down=ffn2_w_down, ln3_g=ln3_g, ln3_b=ln3_b)
    bsz, l, _ = x.shape
    for i in range(ffn1_w_gate.shape[0]):
        x = _layer(x, p, i, tm=min(FFN_TOKENS, bsz * l), lb_gdn=min(GDN_BLOCK, l), lb_gla=min(GLA_BLOCK, l))
    return x
```

```python
import functools

import numpy as np
import jax
import jax.numpy as jnp
from jax import lax
from jax.experimental import pallas as pl
from jax.experimental.pallas import tpu as pltpu

GDN_HEADS = 4
GDN_DK = 128
GDN_DV = 128
CONV_K = 4
GLA_HEADS = 4
GLA_DK = 64
GLA_DV = 128
GLA_RANK = 16
GLA_NORMALIZER = 16.0
CHUNK = 64
DEPTH = 1
ALPHA = (2.0 * DEPTH) ** 0.25
LN_EPS = 1e-5
RMS_EPS = 1e-6

GDN_QK_W = GDN_HEADS * GDN_DK
GDN_V_W = GDN_HEADS * GDN_DV
GLA_QK_W = GLA_HEADS * GLA_DK
GLA_V_W = GLA_HEADS * GLA_DV
CONV_CH = 2 * GDN_QK_W + GDN_V_W

LANES = 128
SUBLANES = 8
MXU_DIM = 256
VMEM_LIMIT_BYTES = 56 * 1024 * 1024

FFN_TOKENS = 1024
GDN_BLOCK = 256
GLA_BLOCK = 1024

SMALL_W = LANES
PROJ_W = CONV_CH + GDN_V_W + 2 * GLA_QK_W + 2 * GLA_V_W + SMALL_W
OFF_QKV = 0
OFF_Z = CONV_CH
OFF_GQ = OFF_Z + GDN_V_W
OFF_GK = OFF_GQ + GLA_QK_W
OFF_GV = OFF_GK + GLA_QK_W
OFF_GG = OFF_GV + GLA_V_W
OFF_SMALL = OFF_GG + GLA_V_W

BF16 = jnp.bfloat16
F32 = jnp.float32


def _dot(a, b):
    return jnp.dot(a.astype(BF16), b.astype(BF16), preferred_element_type=F32)


def _dot_nt(a, b):
    return lax.dot_general(a.astype(BF16), b.astype(BF16), (((1,), (1,)), ((), ())),
                           preferred_element_type=F32)


def _dot_tn(a, b):
    return lax.dot_general(a.astype(BF16), b.astype(BF16), (((0,), (0,)), ((), ())),
                           preferred_element_type=F32)


def _split_bf16(x, pieces):
    out = []
    for _ in range(pieces - 1):
        p = x.astype(BF16)
        out.append(p)
        x = x - p.astype(F32)
    out.append(x.astype(BF16))
    return out


def _sum_dot(sel3, x):
    return jnp.dot(sel3, jnp.concatenate(_split_bf16(x, 3), axis=0), preferred_element_type=F32)


def _dot_split(a, b):
    a_hi, a_lo = _split_bf16(a, 2)
    b_hi, b_lo = _split_bf16(b, 2)
    return jnp.dot(jnp.concatenate([a_hi, a_lo, a_hi], axis=1), jnp.concatenate([b_hi, b_hi, b_lo], axis=0),
                   preferred_element_type=F32)


NEG_LOG2_E = -1.4426950408889634


def _sigmoid(x):
    return 1.0 / (1.0 + jnp.exp2(x * NEG_LOG2_E))


def _silu(x):
    return x * _sigmoid(x)


def _softplus(x):
    return jnp.maximum(x, 0.0) + jnp.log(1.0 + jnp.exp(-jnp.abs(x)))


def _layer_norm_rows(r, g, b):
    mu = jnp.mean(r, axis=-1, keepdims=True)
    c = r - mu
    var = jnp.mean(c * c, axis=-1, keepdims=True)
    return c * lax.rsqrt(var + LN_EPS) * g + b


ROW_SPLITS = 4


def _row_groups(tm):
    step = tm // ROW_SPLITS
    return [slice(r * step, (r + 1) * step) for r in range(ROW_SPLITS)]


def _swiglu_ln(x, wg_ref, wu_ref, wd_ref, g_ref, b_ref, act_ref, o_ref, ff_chunk):
    xb = x.astype(BF16)
    d_ff = wg_ref.shape[1]
    for j in range(d_ff // ff_chunk):
        sl = slice(j * ff_chunk, (j + 1) * ff_chunk)
        gate = jnp.dot(xb, wg_ref[:, sl], preferred_element_type=F32)
        up = jnp.dot(xb, wu_ref[:, sl], preferred_element_type=F32)
        act_ref[:, sl] = (_silu(gate) * up).astype(BF16)
    for rs in _row_groups(x.shape[0]):
        y = jnp.dot(act_ref[rs, :], wd_ref[...], preferred_element_type=F32)
        o_ref[rs, :] = _layer_norm_rows(ALPHA * x[rs, :] + 0.5 * y, g_ref[...], b_ref[...])


def _ffn_ln_kernel(x_ref, wg_ref, wu_ref, wd_ref, g_ref, b_ref, o_ref, act_ref, *, ff_chunk):
    _swiglu_ln(x_ref[...], wg_ref, wu_ref, wd_ref, g_ref, b_ref, act_ref, o_ref, ff_chunk)


def _mix_ffn_ln_kernel(h_ref, oa_ref, ob_ref, wa_ref, wb_ref, g2_ref, b2_ref,
                       wg_ref, wu_ref, wd_ref, g3_ref, b3_ref, o_ref, act_ref, *, ff_chunk):
    xs = []
    for rs in _row_groups(h_ref.shape[0]):
        mix = (jnp.dot(oa_ref[rs, :].astype(BF16), wa_ref[...], preferred_element_type=F32)
               + jnp.dot(ob_ref[rs, :].astype(BF16), wb_ref[...], preferred_element_type=F32))
        xs.append(_layer_norm_rows(ALPHA * h_ref[rs, :] + mix, g2_ref[...], b2_ref[...]))
    _swiglu_ln(jnp.concatenate(xs, axis=0), wg_ref, wu_ref, wd_ref, g3_ref, b3_ref, act_ref, o_ref, ff_chunk)


def _mix_ffn_ln(h2d, oa2d, ob2d, w_a, w_b, ln2_g, ln2_b, wg, wu, wd, ln3_g, ln3_b, *, tm):
    t, d = h2d.shape
    d_ff = wg.shape[1]
    const = lambda i: (0, 0)
    tile = lambda w: pl.BlockSpec((tm, w), lambda i: (i, 0))
    resident = lambda a: pl.BlockSpec(a.shape, const, pipeline_mode=pl.Buffered(1))
    return pl.pallas_call(
        functools.partial(_mix_ffn_ln_kernel, ff_chunk=MXU_DIM),
        grid=(t // tm,),
        in_specs=[tile(d), tile(oa2d.shape[1]), tile(ob2d.shape[1]), resident(w_a), resident(w_b),
                  pl.BlockSpec((1, d), const), pl.BlockSpec((1, d), const),
                  resident(wg), resident(wu), resident(wd),
                  pl.BlockSpec((1, d), const), pl.BlockSpec((1, d), const)],
        out_specs=tile(d),
        out_shape=jax.ShapeDtypeStruct((t, d), F32),
        scratch_shapes=[pltpu.VMEM((tm, d_ff), BF16)],
        compiler_params=pltpu.CompilerParams(
            dimension_semantics=("parallel",), vmem_limit_bytes=VMEM_LIMIT_BYTES),
        name="mix_ffn_ln",
    )(h2d, oa2d, ob2d, w_a, w_b, ln2_g, ln2_b, wg, wu, wd, ln3_g, ln3_b)


def _ffn_ln(x2d, wg, wu, wd, ln_g, ln_b, *, tm):
    t, d = x2d.shape
    d_ff = wg.shape[1]
    const = lambda i: (0, 0)
    return pl.pallas_call(
        functools.partial(_ffn_ln_kernel, ff_chunk=MXU_DIM),
        grid=(t // tm,),
        in_specs=[
            pl.BlockSpec((tm, d), lambda i: (i, 0)),
            pl.BlockSpec((d, d_ff), const, pipeline_mode=pl.Buffered(1)),
            pl.BlockSpec((d, d_ff), const, pipeline_mode=pl.Buffered(1)),
            pl.BlockSpec((d_ff, d), const, pipeline_mode=pl.Buffered(1)),
            pl.BlockSpec((1, d), const),
            pl.BlockSpec((1, d), const),
        ],
        out_specs=pl.BlockSpec((tm, d), lambda i: (i, 0)),
        out_shape=jax.ShapeDtypeStruct((t, d), F32),
        scratch_shapes=[pltpu.VMEM((tm, d_ff), BF16)],
        compiler_params=pltpu.CompilerParams(
            dimension_semantics=("parallel",), vmem_limit_bytes=VMEM_LIMIT_BYTES),
        name="ffn_ln",
    )(x2d, wg, wu, wd, ln_g, ln_b)


def _proj_kernel(h_ref, w_ref, o_ref):
    o_ref[...] = jnp.dot(h_ref[...].astype(BF16), w_ref[...], preferred_element_type=F32)


def _proj(h2d, w, *, tm):
    t, d = h2d.shape
    n = w.shape[1]
    return pl.pallas_call(
        _proj_kernel,
        grid=(t // tm,),
        in_specs=[
            pl.BlockSpec((tm, d), lambda i: (i, 0)),
            pl.BlockSpec((d, n), lambda i: (0, 0), pipeline_mode=pl.Buffered(1)),
        ],
        out_specs=pl.BlockSpec((tm, n), lambda i: (i, 0)),
        out_shape=jax.ShapeDtypeStruct((t, n), F32),
        compiler_params=pltpu.CompilerParams(
            dimension_semantics=("parallel",), vmem_limit_bytes=VMEM_LIMIT_BYTES),
        name="proj",
    )(h2d, w)


def _iota2(shape, dim):
    return lax.broadcasted_iota(jnp.int32, shape, dim)


def _gated_rmsnorm(o, gate, w):
    ms = jnp.mean(o * o, axis=-1, keepdims=True)
    return o * lax.rsqrt(ms + RMS_EPS) * w * _silu(gate)


GDN_ROWS_PER_STEP = 4


def _gdn_kernel(qkv_ref, z_ref, sm_ref, convw_ref, alog_ref, dtb_ref, nw_ref, tril_ref,
                o_ref, xpad_ref, act_ref, s_ref, wq_ref, u_ref, kd_ref, qk_ref, dl_ref, *, lb):
    c = CHUNK
    halo = SUBLANES
    n_chunks = lb // c
    n_rows = qkv_ref.shape[0]
    n_slabs = CONV_CH // LANES
    n_lanes = n_rows * GDN_HEADS
    lanes = [(r, h) for r in range(n_rows) for h in range(GDN_HEADS)]
    chains = [(ci, ln) for ci in range(n_chunks) for ln in range(n_lanes)]
    rows = lambda ci: slice(ci * c, (ci + 1) * c)
    nw = nw_ref[...]

    @pl.when(pl.program_id(1) == 0)
    def _():
        xpad_ref[:, 0:halo, :] = jnp.zeros((n_rows * n_slabs, halo, LANES), F32)
        for ref in (s_ref, wq_ref, u_ref, kd_ref, qk_ref, dl_ref):
            ref[...] = jnp.zeros(ref.shape, ref.dtype)

    state = [s_ref[ln] for ln in range(n_lanes)]
    for ci in range(n_chunks):
        idx = [ci * n_lanes + ln for ln in range(n_lanes)]
        ws = [_dot(wq_ref[i], state[ln]) for ln, i in enumerate(idx)]
        v_new = [u_ref[i] - ws[ln][:c] for ln, i in enumerate(idx)]
        upd = [_dot_tn(kd_ref[i], v_new[ln]) for ln, i in enumerate(idx)]
        intra = [_dot(qk_ref[i], v_new[ln]) for ln, i in enumerate(idx)]
        for ln, i in enumerate(idx):
            r, h = lanes[ln]
            state[ln] = state[ln] * dl_ref[i] + upd[ln]
            zg = z_ref[r, rows(ci), h * GDN_DV:(h + 1) * GDN_DV]
            o_ref[r, rows(ci), h * GDN_DV:(h + 1) * GDN_DV] = _gated_rmsnorm(ws[ln][c:] + intra[ln], zg, nw)
    for ln in range(n_lanes):
        s_ref[ln] = state[ln]

    pitch = (halo + lb) // SUBLANES
    assert pitch * SUBLANES == halo + lb and pitch % 8 != 0
    for r in range(n_rows):
        for j in range(n_slabs):
            sj = r * n_slabs + j
            ls = slice(j * LANES, (j + 1) * LANES)
            xpad_ref[sj, halo:halo + lb, :] = qkv_ref[r, :, ls]
            taps = [jnp.broadcast_to(convw_ref[k:k + 1, ls], (SUBLANES, LANES)) for k in range(CONV_K)]
            tiles = [xpad_ref[sj, pl.ds(g, SUBLANES, stride=pitch), :] for g in range(pitch)]
            wrapped = {-k: pltpu.roll(tiles[pitch - k], 1, 0) for k in range(1, CONV_K)}
            back = lambda g: tiles[g] if g >= 0 else wrapped[g]
            is_qk = j < 2 * GDN_HEADS
            scale = GDN_DK ** -0.5 if j < GDN_HEADS else 1.0
            for g in range(pitch):
                y = taps[CONV_K - 1] * tiles[g]
                for k in range(CONV_K - 1):
                    y = y + taps[k] * back(g - (CONV_K - 1 - k))
                y = _silu(y)
                if is_qk:
                    y = y * (lax.rsqrt(jnp.sum(y * y, axis=-1, keepdims=True) + RMS_EPS) * scale)
                act_ref[sj, pl.ds(g, SUBLANES, stride=pitch), :] = y
            xpad_ref[sj, 0:halo, :] = xpad_ref[sj, lb:lb + halo, :]

    row = _iota2((c, c), 0)
    col = _iota2((c, c), 1)
    causal = row >= col
    strict = row > col
    eye = (row == col).astype(F32)
    tril = tril_ref[...]

    beta_all, g_cum, g_cum_t = [], [], []
    for r in range(n_rows):
        sm = sm_ref[r]
        beta_all.append(_sigmoid(sm))
        g_all = -jnp.exp(alog_ref[...]) * _softplus(sm + dtb_ref[...])
        g_cum.append([_sum_dot(tril, g_all[rows(ci), :]) for ci in range(n_chunks)])
        g_cum_t.append([jnp.transpose(jnp.concatenate([g, g], axis=0))[:, :c] for g in g_cum[r]])

    qs, ks, vs, betas, gcs, decays, e_gs, g_lasts = {}, {}, {}, {}, {}, {}, {}, {}
    for ci, ln in chains:
        r, h = lanes[ln]
        rs = rows(ci)
        ars = slice(halo + ci * c, halo + (ci + 1) * c)
        qs[ci, ln] = act_ref[r * n_slabs + h, ars, :]
        ks[ci, ln] = act_ref[r * n_slabs + GDN_HEADS + h, ars, :]
        vs[ci, ln] = act_ref[r * n_slabs + 2 * GDN_HEADS + h, ars, :]
        betas[ci, ln] = beta_all[r][rs, h:h + 1]
        gc = g_cum[r][ci][:, GDN_HEADS + h:GDN_HEADS + h + 1]
        gr = g_cum_t[r][ci][GDN_HEADS + h:GDN_HEADS + h + 1, :]
        gcs[ci, ln] = gc
        g_lasts[ci, ln] = gc[c - 1:c, :]
        decays[ci, ln] = jnp.where(causal, jnp.exp(jnp.where(causal, gc - gr, 0.0)), 0.0)
        e_gs[ci, ln] = jnp.exp(gc)

    qk_kk = {ch: _dot_nt(jnp.concatenate([qs[ch], ks[ch]], axis=0), ks[ch]) for ch in chains}
    qk = {ch: qk_kk[ch][:c] * decays[ch] for ch in chains}
    a_neg = {ch: jnp.where(strict, -(betas[ch] * qk_kk[ch][c:] * decays[ch]), 0.0) for ch in chains}
    a_bf = {ch: a_neg[ch].astype(BF16) for ch in chains}
    n_pow = {ch: _dot(a_bf[ch], a_bf[ch]).astype(BF16) for ch in chains}
    t_inv = {ch: eye + a_neg[ch] for ch in chains}
    for _ in range(4):
        y = {ch: _dot(jnp.concatenate([n_pow[ch], t_inv[ch].astype(BF16)], axis=0), n_pow[ch]) for ch in chains}
        n_pow = {ch: y[ch][:c].astype(BF16) for ch in chains}
        t_inv = {ch: t_inv[ch] + y[ch][c:] for ch in chains}
    corr = {ch: _dot(t_inv[ch], n_pow[ch]) for ch in chains}
    t_inv = {ch: t_inv[ch] + corr[ch] for ch in chains}

    sol = {ch: _dot(t_inv[ch], jnp.concatenate(
        [vs[ch] * betas[ch], ks[ch] * (betas[ch] * e_gs[ch])], axis=1)) for ch in chains}
    wq = {ch: jnp.concatenate([sol[ch][:, GDN_DV:], qs[ch] * e_gs[ch]], axis=0) for ch in chains}
    k_dec = {ch: ks[ch] * jnp.exp(g_lasts[ch] - gcs[ch]) for ch in chains}

    for ci, ln in chains:
        i = ci * n_lanes + ln
        wq_ref[i] = wq[ci, ln].astype(BF16)
        u_ref[i] = sol[ci, ln][:, :GDN_DV]
        kd_ref[i] = k_dec[ci, ln].astype(BF16)
        qk_ref[i] = qk[ci, ln].astype(BF16)
        dl_ref[i] = jnp.broadcast_to(jnp.exp(g_lasts[ci, ln]), (1, GDN_DV))


def _gdn(proj3d, conv_w, alog_row, dtb_row, norm_w, tril, *, lb):
    b, l, _ = proj3d.shape
    nb = l // lb
    nr = GDN_ROWS_PER_STEP if b % GDN_ROWS_PER_STEP == 0 else 1
    n_chains = (lb // CHUNK) * GDN_HEADS * nr
    n_slabs = CONV_CH // LANES
    const2 = lambda bi, ti: (0, 0)
    prep = lambda ti: jnp.minimum(ti, nb - 1)
    scan = lambda ti: jnp.maximum(ti - 1, 0)
    return pl.pallas_call(
        functools.partial(_gdn_kernel, lb=lb),
        grid=(b // nr, nb + 1),
        in_specs=[
            pl.BlockSpec((nr, lb, CONV_CH), lambda bi, ti: (bi, prep(ti), OFF_QKV // CONV_CH)),
            pl.BlockSpec((nr, lb, GDN_V_W), lambda bi, ti: (bi, scan(ti), OFF_Z // GDN_V_W)),
            pl.BlockSpec((nr, lb, SMALL_W), lambda bi, ti: (bi, prep(ti), OFF_SMALL // SMALL_W)),
            pl.BlockSpec((CONV_K, CONV_CH), const2),
            pl.BlockSpec((1, SMALL_W), const2),
            pl.BlockSpec((1, SMALL_W), const2),
            pl.BlockSpec((1, GDN_DV), const2),
            pl.BlockSpec((CHUNK, 3 * CHUNK), const2),
        ],
        out_specs=pl.BlockSpec((nr, lb, GDN_V_W), lambda bi, ti: (bi, scan(ti), 0)),
        out_shape=jax.ShapeDtypeStruct((b, l, GDN_V_W), F32),
        scratch_shapes=[
            pltpu.VMEM((nr * n_slabs, lb + SUBLANES, LANES), F32),
            pltpu.VMEM((nr * n_slabs, lb + SUBLANES, LANES), F32),
            pltpu.VMEM((nr * GDN_HEADS, GDN_DK, GDN_DV), F32),
            pltpu.VMEM((n_chains, 2 * CHUNK, GDN_DK), BF16),
            pltpu.VMEM((n_chains, CHUNK, GDN_DV), F32),
            pltpu.VMEM((n_chains, CHUNK, GDN_DK), BF16),
            pltpu.VMEM((n_chains, CHUNK, CHUNK), BF16),
            pltpu.VMEM((n_chains, 1, GDN_DV), F32),
        ],
        compiler_params=pltpu.CompilerParams(
            dimension_semantics=("parallel", "arbitrary"), vmem_limit_bytes=VMEM_LIMIT_BYTES),
        name="gdn",
    )(proj3d, proj3d, proj3d, conv_w, alog_row, dtb_row, norm_w, tril)


GLA_LEVELS = (32, 16, 8, 4, 2, 1)


def _gla_sum_matrix():
    c = CHUNK
    r = np.arange(c)[:, None]
    s = np.arange(c)[None, :]
    blocks = [(s <= r), (s > r)]
    for hs in GLA_LEVELS:
        pos = r % (2 * hs)
        m = r - pos + hs
        later = pos >= hs
        blocks.append(np.where(later, (s > m) & (s <= r), (s > r) & (s <= m)))
    return np.concatenate(blocks, axis=0).astype(np.float32)


def _gla_kernel(q_ref, k_ref, v_ref, g_ref, sm_ref, wgk_ref, bgk_ref, nw_ref, lmat_ref,
                o_ref, s_ref, *, lb):
    c = CHUNK

    @pl.when(pl.program_id(1) == 0)
    def _():
        s_ref[...] = jnp.zeros(s_ref.shape, F32)

    lmat = lmat_ref[...]
    nw = nw_ref[...]
    scale = GLA_DK ** -0.5

    gate_pre = _dot_split(sm_ref[0], wgk_ref[...]) + bgk_ref[...]
    logg_all = (jnp.minimum(gate_pre, 0.0) - jnp.log(1.0 + jnp.exp(-jnp.abs(gate_pre)))) / GLA_NORMALIZER

    n_chunks = lb // c
    chains = [(ci, h) for ci in range(n_chunks) for h in range(GLA_HEADS)]
    rows = lambda ci: slice(ci * c, (ci + 1) * c)
    lanes = lambda h: slice(h * GLA_DK, (h + 1) * GLA_DK)

    sums = [_sum_dot(lmat, logg_all[rows(ci), :]) for ci in range(n_chunks)]
    q_in, k_out, e_last, qm, km, qk_diag = [], [], [], [], [], []
    for ci in range(n_chunks):
        q = q_ref[0, rows(ci), :] * scale
        k = k_ref[0, rows(ci), :]
        e_b = jnp.exp(sums[ci][0:c])
        e_last.append(e_b[c - 1:c, :])
        q_in.append(q * e_b)
        k_out.append(k * jnp.exp(sums[ci][c:2 * c]))
        qm_c, km_c = [], []
        for li, hs in enumerate(GLA_LEVELS):
            e_l = jnp.exp(sums[ci][(2 + li) * c:(3 + li) * c])
            qm_c.append(q * e_l)
            km_c.append(k * e_l)
        qm.append(qm_c)
        km.append(km_c)
        qk_diag.append(q * k)

    pair_w = 2 * GLA_DK
    pairs = [(ci, p) for ci in range(n_chunks) for p in range(GLA_HEADS // 2)]
    pair_lanes = lambda p: slice(p * pair_w, (p + 1) * pair_w)
    row2 = _iota2((c, pair_w), 0)
    lane2 = _iota2((c, pair_w), 1)
    first = lane2 < GLA_DK
    col2 = lane2 & (GLA_DK - 1)
    eye2 = row2 == col2

    def pair_keys(x):
        return jnp.concatenate([jnp.where(first, x, 0.0), jnp.where(first, 0.0, x)], axis=0)

    att = {}
    for ci, p in pairs:
        qk_p = qk_diag[ci][:, pair_lanes(p)]
        s_a = jnp.sum(jnp.where(first, qk_p, 0.0), axis=-1, keepdims=True)
        s_b = jnp.sum(jnp.where(first, 0.0, qk_p), axis=-1, keepdims=True)
        att[ci, p] = jnp.where(eye2, jnp.where(first, s_a, s_b), 0.0)
    for li, hs in enumerate(GLA_LEVELS):
        part = {(ci, p): _dot_nt(qm[ci][li][:, pair_lanes(p)], pair_keys(km[ci][li][:, pair_lanes(p)]))
                for ci, p in pairs}
        keep = ((row2 ^ col2) < 2 * hs) & ((row2 & hs) != 0) & ((col2 & hs) == 0)
        for cp in pairs:
            att[cp] = att[cp] + jnp.where(keep, part[cp], 0.0)
    vs = {(ci, h): v_ref[0, rows(ci), h * GLA_DV:(h + 1) * GLA_DV] for ci, h in chains}
    zero_v = jnp.zeros((c, GLA_DV), F32)
    intra_pair = {(ci, p): _dot(att[ci, p], jnp.concatenate(
        [jnp.concatenate([vs[ci, 2 * p], zero_v], axis=1),
         jnp.concatenate([zero_v, vs[ci, 2 * p + 1]], axis=1)], axis=0)) for ci, p in pairs}
    intra = {(ci, h): intra_pair[ci, h // 2][:, (h % 2) * GLA_DV:(h % 2 + 1) * GLA_DV] for ci, h in chains}
    upd ={(ci, h): _dot_tn(vs[ci, h], k_out[ci][:, lanes(h)]) for ci, h in chains}

    state = [s_ref[h] for h in range(GLA_HEADS)]
    for ci in range(n_chunks):
        inter = [_dot_nt(q_in[ci][:, lanes(h)], state[h]) for h in range(GLA_HEADS)]
        for h in range(GLA_HEADS):
            state[h] = state[h] * e_last[ci][:, lanes(h)] + upd[ci, h]
            gg = g_ref[0, rows(ci), h * GLA_DV:(h + 1) * GLA_DV]
            o_ref[0, rows(ci), h * GLA_DV:(h + 1) * GLA_DV] = _gated_rmsnorm(inter[h] + intra[ci, h], gg, nw)
    for h in range(GLA_HEADS):
        s_ref[h] = state[h]


def _gla(proj3d, wgk_pad, bgk_row, norm_w, lmat, *, lb):
    b, l, _ = proj3d.shape
    const2 = lambda bi, ti: (0, 0)
    return pl.pallas_call(
        functools.partial(_gla_kernel, lb=lb),
        grid=(b, l // lb),
        in_specs=[
            pl.BlockSpec((1, lb, GLA_QK_W), lambda bi, ti: (bi, ti, OFF_GQ // GLA_QK_W)),
            pl.BlockSpec((1, lb, GLA_QK_W), lambda bi, ti: (bi, ti, OFF_GK // GLA_QK_W)),
            pl.BlockSpec((1, lb, GLA_V_W), lambda bi, ti: (bi, ti, OFF_GV // GLA_V_W)),
            pl.BlockSpec((1, lb, GLA_V_W), lambda bi, ti: (bi, ti, OFF_GG // GLA_V_W)),
            pl.BlockSpec((1, lb, SMALL_W), lambda bi, ti: (bi, ti, OFF_SMALL // SMALL_W)),
            pl.BlockSpec((SMALL_W, GLA_QK_W), const2),
            pl.BlockSpec((1, GLA_QK_W), const2),
            pl.BlockSpec((1, GLA_DV), const2),
            pl.BlockSpec(lmat.shape, const2),
        ],
        out_specs=pl.BlockSpec((1, lb, GLA_V_W), lambda bi, ti: (bi, ti, 0)),
        out_shape=jax.ShapeDtypeStruct((b, l, GLA_V_W), F32),
        scratch_shapes=[pltpu.VMEM((GLA_HEADS, GLA_DV, GLA_DK), F32)],
        compiler_params=pltpu.CompilerParams(
            dimension_semantics=("parallel", "arbitrary"), vmem_limit_bytes=VMEM_LIMIT_BYTES),
        name="gla",
    )(proj3d, proj3d, proj3d, proj3d, proj3d, wgk_pad, bgk_row, norm_w, lmat)


def _regroup_w_in(w_in):
    d = w_in.shape[0]
    o = 0
    qkv = w_in[:, o:o + CONV_CH]; o += CONV_CH
    z = w_in[:, o:o + GDN_V_W]; o += GDN_V_W
    beta = w_in[:, o:o + GDN_HEADS]; o += GDN_HEADS
    dec = w_in[:, o:o + GDN_HEADS]; o += GDN_HEADS
    gq = w_in[:, o:o + GLA_QK_W]; o += GLA_QK_W
    gk = w_in[:, o:o + GLA_QK_W]; o += GLA_QK_W
    gv = w_in[:, o:o + GLA_V_W]; o += GLA_V_W
    gg = w_in[:, o:o + GLA_V_W]; o += GLA_V_W
    lr = w_in[:, o:o + GLA_RANK]
    small = jnp.concatenate(
        [beta, dec, lr, jnp.zeros((d, SMALL_W - 2 * GDN_HEADS - GLA_RANK), w_in.dtype)], axis=1)
    return jnp.concatenate([qkv, z, gq, gk, gv, gg, small], axis=1)


def _lane_row(vals, offset, width):
    row = jnp.zeros((1, width), F32)
    return row.at[0, offset:offset + vals.shape[0]].set(vals.astype(F32))


def _layer(x, p, i, *, tm, lb_gdn, lb_gla):
    bsz, l, d = x.shape
    t = bsz * l
    row = lambda a: a.reshape(1, -1).astype(F32)

    h = _ffn_ln(x.reshape(t, d), p["ffn1_w_gate"][i].astype(BF16), p["ffn1_w_up"][i].astype(BF16),
                p["ffn1_w_down"][i].astype(BF16), row(p["ln1_g"][i]), row(p["ln1_b"][i]), tm=tm)

    proj = _proj(h, _regroup_w_in(p["w_in"][i].astype(BF16)), tm=tm).reshape(bsz, l, PROJ_W)

    tril = jnp.asarray(np.tile(np.tril(np.ones((CHUNK, CHUNK), np.float32)), (1, 3)), BF16)
    o_a = _gdn(proj, p["conv_w"][i].astype(F32),
               _lane_row(p["a_log"][i], GDN_HEADS, SMALL_W), _lane_row(p["dt_bias"][i], GDN_HEADS, SMALL_W),
               row(p["gdn_norm_w"][i]), tril, lb=lb_gdn)

    wgk_pad = jnp.zeros((SMALL_W, GLA_QK_W), F32).at[2 * GDN_HEADS:2 * GDN_HEADS + GLA_RANK, :].set(
        p["w_gk"][i].astype(F32))
    o_b = _gla(proj, wgk_pad, row(p["b_gk"][i]), row(p["gla_norm_w"][i]),
               jnp.asarray(np.tile(_gla_sum_matrix(), (1, 3)), BF16), lb=lb_gla)

    w_out = p["w_out"][i].astype(BF16)
    out = _mix_ffn_ln(h, o_a.reshape(t, GDN_V_W), o_b.reshape(t, GLA_V_W), w_out[:GDN_V_W], w_out[GDN_V_W:],
                      row(p["ln2_g"][i]), row(p["ln2_b"][i]),
                      p["ffn2_w_gate"][i].astype(BF16), p["ffn2_w_up"][i].astype(BF16),
                      p["ffn2_w_down"][i].astype(BF16), row(p["ln3_g"][i]), row(p["ln3_b"][i]), tm=tm)
    return out.reshape(bsz, l, d)


def kernel(x, ffn1_w_gate, ffn1_w_up, ffn1_w_down, ln1_g, ln1_b, w_in, conv_w, a_log, dt_bias, gdn_norm_w,
           w_gk, b_gk, gla_norm_w, w_out, ln2_g, ln2_b, ffn2_w_gate, ffn2_w_up, ffn2_w_down, ln3_g, ln3_b):
    p = dict(ffn1_w_gate=ffn1_w_gate, ffn1_w_up=ffn1_w_up, ffn1_w_down=ffn1_w_down, ln1_g=ln1_g, ln1_b=ln1_b,
             w_in=w_in, conv_w=conv_w, a_log=a_log, dt_bias=dt_bias, gdn_norm_w=gdn_norm_w,
             w_gk=w_gk, b_gk=b_gk, gla_norm_w=gla_norm_w, w_out=w_out, ln2_g=ln2_g, ln2_b=ln2_b,
             ffn2_w_gate=ffn2_w_gate, ffn2_w_up=ffn2_w_up, ffn2_w_down=ffn2_w_down, ln3_g=ln3_g, ln3_b=ln3_b)
    bsz, l, _ = x.shape
    for i in range(ffn1_w_gate.shape[0]):
        x = _layer(x, p, i, tm=min(FFN_TOKENS, bsz * l), lb_gdn=min(GDN_BLOCK, l), lb_gla=min(GLA_BLOCK, l))
    return x
```

```python
import functools

import numpy as np
import jax
import jax.numpy as jnp
from jax import lax
from jax.experimental import pallas as pl
from jax.experimental.pallas import tpu as pltpu

GDN_HEADS = 4
GDN_DK = 128
GDN_DV = 128
CONV_K = 4
GLA_HEADS = 4
GLA_DK = 64
GLA_DV = 128
GLA_RANK = 16
GLA_NORMALIZER = 16.0
CHUNK = 64
DEPTH = 1
ALPHA = (2.0 * DEPTH) ** 0.25
LN_EPS = 1e-5
RMS_EPS = 1e-6

GDN_QK_W = GDN_HEADS * GDN_DK
GDN_V_W = GDN_HEADS * GDN_DV
GLA_QK_W = GLA_HEADS * GLA_DK
GLA_V_W = GLA_HEADS * GLA_DV
CONV_CH = 2 * GDN_QK_W + GDN_V_W

LANES = 128
SUBLANES = 8
MXU_DIM = 256
VMEM_LIMIT_BYTES = 56 * 1024 * 1024

FFN_TOKENS = 1024
GDN_BLOCK = 256
GLA_BLOCK = 1024

SMALL_W = LANES
OFF_QKV = 0
OFF_Z = CONV_CH
OFF_SMALL = OFF_Z + GDN_V_W
GDN_PROJ_W = OFF_SMALL + SMALL_W
OFF_GQ = 0
OFF_GK = OFF_GQ + GLA_QK_W
OFF_GV = OFF_GK + GLA_QK_W
OFF_GG = OFF_GV + GLA_V_W
OFF_GSMALL = OFF_GG + GLA_V_W
GLA_PROJ_W = OFF_GSMALL + SMALL_W

BF16 = jnp.bfloat16
F32 = jnp.float32


def _dot(a, b):
    return jnp.dot(a.astype(BF16), b.astype(BF16), preferred_element_type=F32)


def _dot_nt(a, b):
    return lax.dot_general(a.astype(BF16), b.astype(BF16), (((1,), (1,)), ((), ())),
                           preferred_element_type=F32)


def _dot_tn(a, b):
    return lax.dot_general(a.astype(BF16), b.astype(BF16), (((0,), (0,)), ((), ())),
                           preferred_element_type=F32)


def _split_bf16(x, pieces):
    out = []
    for _ in range(pieces - 1):
        p = x.astype(BF16)
        out.append(p)
        x = x - p.astype(F32)
    out.append(x.astype(BF16))
    return out


def _sum_dot(sel3, x):
    return jnp.dot(sel3, jnp.concatenate(_split_bf16(x, 3), axis=0), preferred_element_type=F32)


def _dot_split(a, b):
    a_hi, a_lo = _split_bf16(a, 2)
    b_hi, b_lo = _split_bf16(b, 2)
    return jnp.dot(jnp.concatenate([a_hi, a_lo, a_hi], axis=1), jnp.concatenate([b_hi, b_hi, b_lo], axis=0),
                   preferred_element_type=F32)


NEG_LOG2_E = -1.4426950408889634


def _sigmoid(x):
    return 1.0 / (1.0 + jnp.exp2(x * NEG_LOG2_E))


def _silu(x):
    return x * _sigmoid(x)


def _softplus(x):
    return jnp.maximum(x, 0.0) + jnp.log(1.0 + jnp.exp(-jnp.abs(x)))


def _layer_norm_rows(r, g, b):
    mu = jnp.mean(r, axis=-1, keepdims=True)
    c = r - mu
    var = jnp.mean(c * c, axis=-1, keepdims=True)
    return c * lax.rsqrt(var + LN_EPS) * g + b


ROW_SPLITS = 4


def _row_groups(tm):
    step = tm // ROW_SPLITS
    return [slice(r * step, (r + 1) * step) for r in range(ROW_SPLITS)]


def _swiglu_ln(x, wg_ref, wu_ref, wd_ref, g_ref, b_ref, act_ref, o_ref, ff_chunk):
    xb = x.astype(BF16)
    d_ff = wg_ref.shape[1]
    for j in range(d_ff // ff_chunk):
        sl = slice(j * ff_chunk, (j + 1) * ff_chunk)
        gate = jnp.dot(xb, wg_ref[:, sl], preferred_element_type=F32)
        up = jnp.dot(xb, wu_ref[:, sl], preferred_element_type=F32)
        act_ref[:, sl] = (_silu(gate) * up).astype(BF16)
    for rs in _row_groups(x.shape[0]):
        y = jnp.dot(act_ref[rs, :], wd_ref[...], preferred_element_type=F32)
        o_ref[rs, :] = _layer_norm_rows(ALPHA * x[rs, :] + 0.5 * y, g_ref[...], b_ref[...])


def _ffn_ln_kernel(x_ref, wg_ref, wu_ref, wd_ref, g_ref, b_ref, o_ref, act_ref, *, ff_chunk):
    _swiglu_ln(x_ref[...], wg_ref, wu_ref, wd_ref, g_ref, b_ref, act_ref, o_ref, ff_chunk)


def _mix_ffn_ln_kernel(h_ref, oa_ref, ob_ref, wa_ref, wb_ref, g2_ref, b2_ref,
                       wg_ref, wu_ref, wd_ref, g3_ref, b3_ref, o_ref, act_ref, *, ff_chunk):
    xs = []
    for rs in _row_groups(h_ref.shape[0]):
        mix = (jnp.dot(oa_ref[rs, :].astype(BF16), wa_ref[...], preferred_element_type=F32)
               + jnp.dot(ob_ref[rs, :].astype(BF16), wb_ref[...], preferred_element_type=F32))
        xs.append(_layer_norm_rows(ALPHA * h_ref[rs, :] + mix, g2_ref[...], b2_ref[...]))
    _swiglu_ln(jnp.concatenate(xs, axis=0), wg_ref, wu_ref, wd_ref, g3_ref, b3_ref, act_ref, o_ref, ff_chunk)


def _mix_ffn_ln(h2d, oa2d, ob2d, w_a, w_b, ln2_g, ln2_b, wg, wu, wd, ln3_g, ln3_b, *, tm):
    t, d = h2d.shape
    d_ff = wg.shape[1]
    const = lambda i: (0, 0)
    tile = lambda w: pl.BlockSpec((tm, w), lambda i: (i, 0))
    resident = lambda a: pl.BlockSpec(a.shape, const, pipeline_mode=pl.Buffered(1))
    return pl.pallas_call(
        functools.partial(_mix_ffn_ln_kernel, ff_chunk=MXU_DIM),
        grid=(t // tm,),
        in_specs=[tile(d), tile(oa2d.shape[1]), tile(ob2d.shape[1]), resident(w_a), resident(w_b),
                  pl.BlockSpec((1, d), const), pl.BlockSpec((1, d), const),
                  resident(wg), resident(wu), resident(wd),
                  pl.BlockSpec((1, d), const), pl.BlockSpec((1, d), const)],
        out_specs=tile(d),
        out_shape=jax.ShapeDtypeStruct((t, d), F32),
        scratch_shapes=[pltpu.VMEM((tm, d_ff), BF16)],
        compiler_params=pltpu.CompilerParams(
            dimension_semantics=("parallel",), vmem_limit_bytes=VMEM_LIMIT_BYTES),
        name="mix_ffn_ln",
    )(h2d, oa2d, ob2d, w_a, w_b, ln2_g, ln2_b, wg, wu, wd, ln3_g, ln3_b)


def _ffn_ln(x2d, wg, wu, wd, ln_g, ln_b, *, tm):
    t, d = x2d.shape
    d_ff = wg.shape[1]
    const = lambda i: (0, 0)
    return pl.pallas_call(
        functools.partial(_ffn_ln_kernel, ff_chunk=MXU_DIM),
        grid=(t // tm,),
        in_specs=[
            pl.BlockSpec((tm, d), lambda i: (i, 0)),
            pl.BlockSpec((d, d_ff), const, pipeline_mode=pl.Buffered(1)),
            pl.BlockSpec((d, d_ff), const, pipeline_mode=pl.Buffered(1)),
            pl.BlockSpec((d_ff, d), const, pipeline_mode=pl.Buffered(1)),
            pl.BlockSpec((1, d), const),
            pl.BlockSpec((1, d), const),
        ],
        out_specs=pl.BlockSpec((tm, d), lambda i: (i, 0)),
        out_shape=jax.ShapeDtypeStruct((t, d), F32),
        scratch_shapes=[pltpu.VMEM((tm, d_ff), BF16)],
        compiler_params=pltpu.CompilerParams(
            dimension_semantics=("parallel",), vmem_limit_bytes=VMEM_LIMIT_BYTES),
        name="ffn_ln",
    )(x2d, wg, wu, wd, ln_g, ln_b)


def _proj_kernel(h_ref, w_ref, gdn_ref, gla_ref):
    proj = jnp.dot(h_ref[...].astype(BF16), w_ref[...], preferred_element_type=F32)
    gdn_ref[...] = proj[:, :GDN_PROJ_W]
    gla_ref[...] = proj[:, GDN_PROJ_W:]


def _proj(h2d, w, *, tm):
    t, d = h2d.shape
    n = w.shape[1]
    return pl.pallas_call(
        _proj_kernel,
        grid=(t // tm,),
        in_specs=[
            pl.BlockSpec((tm, d), lambda i: (i, 0)),
            pl.BlockSpec((d, n), lambda i: (0, 0), pipeline_mode=pl.Buffered(1)),
        ],
        out_specs=[pl.BlockSpec((tm, GDN_PROJ_W), lambda i: (i, 0)),
                   pl.BlockSpec((tm, GLA_PROJ_W), lambda i: (i, 0))],
        out_shape=[jax.ShapeDtypeStruct((t, GDN_PROJ_W), F32), jax.ShapeDtypeStruct((t, GLA_PROJ_W), F32)],
        compiler_params=pltpu.CompilerParams(
            dimension_semantics=("parallel",), vmem_limit_bytes=VMEM_LIMIT_BYTES),
        name="proj",
    )(h2d, w)


def _iota2(shape, dim):
    return lax.broadcasted_iota(jnp.int32, shape, dim)


def _gated_rmsnorm(o, gate, w):
    ms = jnp.mean(o * o, axis=-1, keepdims=True)
    return o * lax.rsqrt(ms + RMS_EPS) * w * _silu(gate)


GDN_ROWS_PER_STEP = 4


def _gdn_kernel(qkv_ref, z_ref, sm_ref, convw_ref, alog_ref, dtb_ref, nw_ref, tril_ref,
                o_ref, xpad_ref, act_ref, s_ref, wq_ref, u_ref, kd_ref, qk_ref, dl_ref, *, lb):
    c = CHUNK
    halo = SUBLANES
    n_chunks = lb // c
    n_rows = qkv_ref.shape[0]
    n_slabs = CONV_CH // LANES
    n_lanes = n_rows * GDN_HEADS
    lanes = [(r, h) for r in range(n_rows) for h in range(GDN_HEADS)]
    chains = [(ci, ln) for ci in range(n_chunks) for ln in range(n_lanes)]
    rows = lambda ci: slice(ci * c, (ci + 1) * c)
    nw = nw_ref[...]

    @pl.when(pl.program_id(1) == 0)
    def _():
        xpad_ref[:, 0:halo, :] = jnp.zeros((n_rows * n_slabs, halo, LANES), F32)
        for ref in (s_ref, wq_ref, u_ref, kd_ref, qk_ref, dl_ref):
            ref[...] = jnp.zeros(ref.shape, ref.dtype)

    state = [s_ref[ln] for ln in range(n_lanes)]
    for ci in range(n_chunks):
        idx = [ci * n_lanes + ln for ln in range(n_lanes)]
        ws = [_dot(wq_ref[i], state[ln]) for ln, i in enumerate(idx)]
        v_new = [u_ref[i] - ws[ln][:c] for ln, i in enumerate(idx)]
        upd = [_dot_tn(kd_ref[i], v_new[ln]) for ln, i in enumerate(idx)]
        intra = [_dot(qk_ref[i], v_new[ln]) for ln, i in enumerate(idx)]
        for ln, i in enumerate(idx):
            r, h = lanes[ln]
            state[ln] = state[ln] * dl_ref[i] + upd[ln]
            zg = z_ref[r, rows(ci), h * GDN_DV:(h + 1) * GDN_DV]
            o_ref[r, rows(ci), h * GDN_DV:(h + 1) * GDN_DV] = _gated_rmsnorm(ws[ln][c:] + intra[ln], zg, nw)
    for ln in range(n_lanes):
        s_ref[ln] = state[ln]

    pitch = (halo + lb) // SUBLANES
    assert pitch * SUBLANES == halo + lb and pitch % 8 != 0
    for r in range(n_rows):
        for j in range(n_slabs):
            sj = r * n_slabs + j
            ls = slice(j * LANES, (j + 1) * LANES)
            xpad_ref[sj, halo:halo + lb, :] = qkv_ref[r, :, ls]
            taps = [jnp.broadcast_to(convw_ref[k:k + 1, ls], (SUBLANES, LANES)) for k in range(CONV_K)]
            tiles = [xpad_ref[sj, pl.ds(g, SUBLANES, stride=pitch), :] for g in range(pitch)]
            wrapped = {-k: pltpu.roll(tiles[pitch - k], 1, 0) for k in range(1, CONV_K)}
            back = lambda g: tiles[g] if g >= 0 else wrapped[g]
            is_qk = j < 2 * GDN_HEADS
            scale = GDN_DK ** -0.5 if j < GDN_HEADS else 1.0
            for g in range(pitch):
                y = taps[CONV_K - 1] * tiles[g]
                for k in range(CONV_K - 1):
                    y = y + taps[k] * back(g - (CONV_K - 1 - k))
                y = _silu(y)
                if is_qk:
                    y = y * (lax.rsqrt(jnp.sum(y * y, axis=-1, keepdims=True) + RMS_EPS) * scale)
                act_ref[sj, pl.ds(g, SUBLANES, stride=pitch), :] = y
            xpad_ref[sj, 0:halo, :] = xpad_ref[sj, lb:lb + halo, :]

    row = _iota2((c, c), 0)
    col = _iota2((c, c), 1)
    causal = row >= col
    strict = row > col
    eye = (row == col).astype(F32)
    tril = tril_ref[...]

    beta_all, g_cum, g_cum_t = [], [], []
    for r in range(n_rows):
        sm = sm_ref[r]
        beta_all.append(_sigmoid(sm))
        g_all = -jnp.exp(alog_ref[...]) * _softplus(sm + dtb_ref[...])
        g_cum.append([_sum_dot(tril, g_all[rows(ci), :]) for ci in range(n_chunks)])
        g_cum_t.append([jnp.transpose(jnp.concatenate([g, g], axis=0))[:, :c] for g in g_cum[r]])

    qs, ks, vs, betas, gcs, decays, e_gs, g_lasts = {}, {}, {}, {}, {}, {}, {}, {}
    for ci, ln in chains:
        r, h = lanes[ln]
        rs = rows(ci)
        ars = slice(halo + ci * c, halo + (ci + 1) * c)
        qs[ci, ln] = act_ref[r * n_slabs + h, ars, :]
        ks[ci, ln] = act_ref[r * n_slabs + GDN_HEADS + h, ars, :]
        vs[ci, ln] = act_ref[r * n_slabs + 2 * GDN_HEADS + h, ars, :]
        betas[ci, ln] = beta_all[r][rs, h:h + 1]
        gc = g_cum[r][ci][:, GDN_HEADS + h:GDN_HEADS + h + 1]
        gr = g_cum_t[r][ci][GDN_HEADS + h:GDN_HEADS + h + 1, :]
        gcs[ci, ln] = gc
        g_lasts[ci, ln] = gc[c - 1:c, :]
        decays[ci, ln] = jnp.where(causal, jnp.exp(jnp.where(causal, gc - gr, 0.0)), 0.0)
        e_gs[ci, ln] = jnp.exp(gc)

    qk_kk = {ch: _dot_nt(jnp.concatenate([qs[ch], ks[ch]], axis=0), ks[ch]) for ch in chains}
    qk = {ch: qk_kk[ch][:c] * decays[ch] for ch in chains}
    a_neg = {ch: jnp.where(strict, -(betas[ch] * qk_kk[ch][c:] * decays[ch]), 0.0) for ch in chains}
    a_bf = {ch: a_neg[ch].astype(BF16) for ch in chains}
    n_pow = {ch: _dot(a_bf[ch], a_bf[ch]).astype(BF16) for ch in chains}
    t_inv = {ch: eye + a_neg[ch] for ch in chains}
    for _ in range(4):
        y = {ch: _dot(jnp.concatenate([n_pow[ch], t_inv[ch].astype(BF16)], axis=0), n_pow[ch]) for ch in chains}
        n_pow = {ch: y[ch][:c].astype(BF16) for ch in chains}
        t_inv = {ch: t_inv[ch] + y[ch][c:] for ch in chains}
    corr = {ch: _dot(t_inv[ch], n_pow[ch]) for ch in chains}
    t_inv = {ch: t_inv[ch] + corr[ch] for ch in chains}

    sol = {ch: _dot(t_inv[ch], jnp.concatenate(
        [vs[ch] * betas[ch], ks[ch] * (betas[ch] * e_gs[ch])], axis=1)) for ch in chains}
    wq = {ch: jnp.concatenate([sol[ch][:, GDN_DV:], qs[ch] * e_gs[ch]], axis=0) for ch in chains}
    k_dec = {ch: ks[ch] * jnp.exp(g_lasts[ch] - gcs[ch]) for ch in chains}

    for ci, ln in chains:
        i = ci * n_lanes + ln
        wq_ref[i] = wq[ci, ln].astype(BF16)
        u_ref[i] = sol[ci, ln][:, :GDN_DV]
        kd_ref[i] = k_dec[ci, ln].astype(BF16)
        qk_ref[i] = qk[ci, ln].astype(BF16)
        dl_ref[i] = jnp.broadcast_to(jnp.exp(g_lasts[ci, ln]), (1, GDN_DV))


def _gdn(proj3d, conv_w, alog_row, dtb_row, norm_w, tril, *, lb):
    b, l, _ = proj3d.shape
    nb = l // lb
    nr = GDN_ROWS_PER_STEP if b % GDN_ROWS_PER_STEP == 0 else 1
    n_chains = (lb // CHUNK) * GDN_HEADS * nr
    n_slabs = CONV_CH // LANES
    const2 = lambda bi, ti: (0, 0)
    prep = lambda ti: jnp.minimum(ti, nb - 1)
    scan = lambda ti: jnp.maximum(ti - 1, 0)
    return pl.pallas_call(
        functools.partial(_gdn_kernel, lb=lb),
        grid=(b // nr, nb + 1),
        in_specs=[
            pl.BlockSpec((nr, lb, CONV_CH), lambda bi, ti: (bi, prep(ti), OFF_QKV // CONV_CH)),
            pl.BlockSpec((nr, lb, GDN_V_W), lambda bi, ti: (bi, scan(ti), OFF_Z // GDN_V_W)),
            pl.BlockSpec((nr, lb, SMALL_W), lambda bi, ti: (bi, prep(ti), OFF_SMALL // SMALL_W)),
            pl.BlockSpec((CONV_K, CONV_CH), const2),
            pl.BlockSpec((1, SMALL_W), const2),
            pl.BlockSpec((1, SMALL_W), const2),
            pl.BlockSpec((1, GDN_DV), const2),
            pl.BlockSpec((CHUNK, 3 * CHUNK), const2),
        ],
        out_specs=pl.BlockSpec((nr, lb, GDN_V_W), lambda bi, ti: (bi, scan(ti), 0)),
        out_shape=jax.ShapeDtypeStruct((b, l, GDN_V_W), F32),
        scratch_shapes=[
            pltpu.VMEM((nr * n_slabs, lb + SUBLANES, LANES), F32),
            pltpu.VMEM((nr * n_slabs, lb + SUBLANES, LANES), F32),
            pltpu.VMEM((nr * GDN_HEADS, GDN_DK, GDN_DV), F32),
            pltpu.VMEM((n_chains, 2 * CHUNK, GDN_DK), BF16),
            pltpu.VMEM((n_chains, CHUNK, GDN_DV), F32),
            pltpu.VMEM((n_chains, CHUNK, GDN_DK), BF16),
            pltpu.VMEM((n_chains, CHUNK, CHUNK), BF16),
            pltpu.VMEM((n_chains, 1, GDN_DV), F32),
        ],
        compiler_params=pltpu.CompilerParams(
            dimension_semantics=("parallel", "arbitrary"), vmem_limit_bytes=VMEM_LIMIT_BYTES),
        name="gdn",
    )(proj3d, proj3d, proj3d, conv_w, alog_row, dtb_row, norm_w, tril)


GLA_LEVELS = (32, 16, 8, 4, 2, 1)


def _gla_sum_matrix():
    c = CHUNK
    r = np.arange(c)[:, None]
    s = np.arange(c)[None, :]
    blocks = [(s <= r), (s > r)]
    for hs in GLA_LEVELS:
        pos = r % (2 * hs)
        m = r - pos + hs
        later = pos >= hs
        blocks.append(np.where(later, (s > m) & (s <= r), (s > r) & (s <= m)))
    return np.concatenate(blocks, axis=0).astype(np.float32)


def _gla_kernel(x_ref, wgk_ref, bgk_ref, nw_ref, lmat_ref,
                o_ref, s_ref, *, lb):
    c = CHUNK

    @pl.when(pl.program_id(1) == 0)
    def _():
        s_ref[...] = jnp.zeros(s_ref.shape, F32)

    lmat = lmat_ref[...]
    nw = nw_ref[...]
    scale = GLA_DK ** -0.5

    gate_pre = _dot_split(x_ref[0, :, OFF_GSMALL:OFF_GSMALL + SMALL_W], wgk_ref[...]) + bgk_ref[...]
    logg_all = (jnp.minimum(gate_pre, 0.0) - jnp.log(1.0 + jnp.exp(-jnp.abs(gate_pre)))) / GLA_NORMALIZER

    n_chunks = lb // c
    chains = [(ci, h) for ci in range(n_chunks) for h in range(GLA_HEADS)]
    rows = lambda ci: slice(ci * c, (ci + 1) * c)
    lanes = lambda h: slice(h * GLA_DK, (h + 1) * GLA_DK)

    sums = [_sum_dot(lmat, logg_all[rows(ci), :]) for ci in range(n_chunks)]
    q_in, k_out, e_last, qm, km, qk_diag = [], [], [], [], [], []
    for ci in range(n_chunks):
        q = x_ref[0, rows(ci), OFF_GQ:OFF_GQ + GLA_QK_W] * scale
        k = x_ref[0, rows(ci), OFF_GK:OFF_GK + GLA_QK_W]
        e_b = jnp.exp(sums[ci][0:c])
        e_last.append(e_b[c - 1:c, :])
        q_in.append(q * e_b)
        k_out.append(k * jnp.exp(sums[ci][c:2 * c]))
        qm_c, km_c = [], []
        for li, hs in enumerate(GLA_LEVELS):
            e_l = jnp.exp(sums[ci][(2 + li) * c:(3 + li) * c])
            qm_c.append(q * e_l)
            km_c.append(k * e_l)
        qm.append(qm_c)
        km.append(km_c)
        qk_diag.append(q * k)

    pair_w = 2 * GLA_DK
    pairs = [(ci, p) for ci in range(n_chunks) for p in range(GLA_HEADS // 2)]
    pair_lanes = lambda p: slice(p * pair_w, (p + 1) * pair_w)
    row2 = _iota2((c, pair_w), 0)
    lane2 = _iota2((c, pair_w), 1)
    first = lane2 < GLA_DK
    col2 = lane2 & (GLA_DK - 1)
    eye2 = row2 == col2

    def pair_keys(x):
        return jnp.concatenate([jnp.where(first, x, 0.0), jnp.where(first, 0.0, x)], axis=0)

    att = {}
    for ci, p in pairs:
        qk_p = qk_diag[ci][:, pair_lanes(p)]
        s_a = jnp.sum(jnp.where(first, qk_p, 0.0), axis=-1, keepdims=True)
        s_b = jnp.sum(jnp.where(first, 0.0, qk_p), axis=-1, keepdims=True)
        att[ci, p] = jnp.where(eye2, jnp.where(first, s_a, s_b), 0.0)
    for li, hs in enumerate(GLA_LEVELS):
        part = {(ci, p): _dot_nt(qm[ci][li][:, pair_lanes(p)], pair_keys(km[ci][li][:, pair_lanes(p)]))
                for ci, p in pairs}
        keep = ((row2 ^ col2) < 2 * hs) & ((row2 & hs) != 0) & ((col2 & hs) == 0)
        for cp in pairs:
            att[cp] = att[cp] + jnp.where(keep, part[cp], 0.0)
    vs = {(ci, h): x_ref[0, rows(ci), OFF_GV + h * GLA_DV:OFF_GV + (h + 1) * GLA_DV] for ci, h in chains}
    zero_v = jnp.zeros((c, GLA_DV), F32)
    intra_pair = {(ci, p): _dot(att[ci, p], jnp.concatenate(
        [jnp.concatenate([vs[ci, 2 * p], zero_v], axis=1),
         jnp.concatenate([zero_v, vs[ci, 2 * p + 1]], axis=1)], axis=0)) for ci, p in pairs}
    intra = {(ci, h): intra_pair[ci, h // 2][:, (h % 2) * GLA_DV:(h % 2 + 1) * GLA_DV] for ci, h in chains}
    upd ={(ci, h): _dot_tn(vs[ci, h], k_out[ci][:, lanes(h)]) for ci, h in chains}

    state = [s_ref[h] for h in range(GLA_HEADS)]
    for ci in range(n_chunks):
        inter = [_dot_nt(q_in[ci][:, lanes(h)], state[h]) for h in range(GLA_HEADS)]
        for h in range(GLA_HEADS):
            state[h] = state[h] * e_last[ci][:, lanes(h)] + upd[ci, h]
            gg = x_ref[0, rows(ci), OFF_GG + h * GLA_DV:OFF_GG + (h + 1) * GLA_DV]
            o_ref[0, rows(ci), h * GLA_DV:(h + 1) * GLA_DV] = _gated_rmsnorm(inter[h] + intra[ci, h], gg, nw)
    for h in range(GLA_HEADS):
        s_ref[h] = state[h]


def _gla(proj3d, wgk_pad, bgk_row, norm_w, lmat, *, lb):
    b, l, _ = proj3d.shape
    const2 = lambda bi, ti: (0, 0)
    return pl.pallas_call(
        functools.partial(_gla_kernel, lb=lb),
        grid=(b, l // lb),
        in_specs=[
            pl.BlockSpec((1, lb, GLA_PROJ_W), lambda bi, ti: (bi, ti, 0)),
            pl.BlockSpec((SMALL_W, GLA_QK_W), const2),
            pl.BlockSpec((1, GLA_QK_W), const2),
            pl.BlockSpec((1, GLA_DV), const2),
            pl.BlockSpec(lmat.shape, const2),
        ],
        out_specs=pl.BlockSpec((1, lb, GLA_V_W), lambda bi, ti: (bi, ti, 0)),
        out_shape=jax.ShapeDtypeStruct((b, l, GLA_V_W), F32),
        scratch_shapes=[pltpu.VMEM((GLA_HEADS, GLA_DV, GLA_DK), F32)],
        compiler_params=pltpu.CompilerParams(
            dimension_semantics=("parallel", "arbitrary"), vmem_limit_bytes=VMEM_LIMIT_BYTES),
        name="gla",
    )(proj3d, wgk_pad, bgk_row, norm_w, lmat)


def _regroup_w_in(w_in):
    d = w_in.shape[0]
    o = 0
    qkv = w_in[:, o:o + CONV_CH]; o += CONV_CH
    z = w_in[:, o:o + GDN_V_W]; o += GDN_V_W
    beta = w_in[:, o:o + GDN_HEADS]; o += GDN_HEADS
    dec = w_in[:, o:o + GDN_HEADS]; o += GDN_HEADS
    gq = w_in[:, o:o + GLA_QK_W]; o += GLA_QK_W
    gk = w_in[:, o:o + GLA_QK_W]; o += GLA_QK_W
    gv = w_in[:, o:o + GLA_V_W]; o += GLA_V_W
    gg = w_in[:, o:o + GLA_V_W]; o += GLA_V_W
    lr = w_in[:, o:o + GLA_RANK]
    small = jnp.concatenate(
        [beta, dec, lr, jnp.zeros((d, SMALL_W - 2 * GDN_HEADS - GLA_RANK), w_in.dtype)], axis=1)
    return jnp.concatenate([qkv, z, small, gq, gk, gv, gg, small], axis=1)


def _lane_row(vals, offset, width):
    row = jnp.zeros((1, width), F32)
    return row.at[0, offset:offset + vals.shape[0]].set(vals.astype(F32))


def _layer(x, p, i, *, tm, lb_gdn, lb_gla):
    bsz, l, d = x.shape
    t = bsz * l
    row = lambda a: a.reshape(1, -1).astype(F32)

    h = _ffn_ln(x.reshape(t, d), p["ffn1_w_gate"][i].astype(BF16), p["ffn1_w_up"][i].astype(BF16),
                p["ffn1_w_down"][i].astype(BF16), row(p["ln1_g"][i]), row(p["ln1_b"][i]), tm=tm)

    proj_gdn, proj_gla = _proj(h, _regroup_w_in(p["w_in"][i].astype(BF16)), tm=tm)
    proj_gdn = proj_gdn.reshape(bsz, l, GDN_PROJ_W)
    proj_gla = proj_gla.reshape(bsz, l, GLA_PROJ_W)

    tril = jnp.asarray(np.tile(np.tril(np.ones((CHUNK, CHUNK), np.float32)), (1, 3)), BF16)
    o_a = _gdn(proj_gdn, p["conv_w"][i].astype(F32),
               _lane_row(p["a_log"][i], GDN_HEADS, SMALL_W), _lane_row(p["dt_bias"][i], GDN_HEADS, SMALL_W),
               row(p["gdn_norm_w"][i]), tril, lb=lb_gdn)

    wgk_pad = jnp.zeros((SMALL_W, GLA_QK_W), F32).at[2 * GDN_HEADS:2 * GDN_HEADS + GLA_RANK, :].set(
        p["w_gk"][i].astype(F32))
    o_b = _gla(proj_gla, wgk_pad, row(p["b_gk"][i]), row(p["gla_norm_w"][i]),
               jnp.asarray(np.tile(_gla_sum_matrix(), (1, 3)), BF16), lb=lb_gla)

    w_out = p["w_out"][i].astype(BF16)
    out = _mix_ffn_ln(h, o_a.reshape(t, GDN_V_W), o_b.reshape(t, GLA_V_W), w_out[:GDN_V_W], w_out[GDN_V_W:],
                      row(p["ln2_g"][i]), row(p["ln2_b"][i]),
                      p["ffn2_w_gate"][i].astype(BF16), p["ffn2_w_up"][i].astype(BF16),
                      p["ffn2_w_down"][i].astype(BF16), row(p["ln3_g"][i]), row(p["ln3_b"][i]), tm=tm)
    return out.reshape(bsz, l, d)


def kernel(x, ffn1_w_gate, ffn1_w_up, ffn1_w_down, ln1_g, ln1_b, w_in, conv_w, a_log, dt_bias, gdn_norm_w,
           w_gk, b_gk, gla_norm_w, w_out, ln2_g, ln2_b, ffn2_w_gate, ffn2_w_up, ffn2_w_down, ln3_g, ln3_b):
    p = dict(ffn1_w_gate=ffn1_w_gate, ffn1_w_up=ffn1_w_up, ffn1_w_down=ffn1_w_down, ln1_g=ln1_g, ln1_b=ln1_b,
             w_in=w_in, conv_w=conv_w, a_log=a_log, dt_bias=dt_bias, gdn_norm_w=gdn_norm_w,
             w_gk=w_gk, b_gk=b_gk, gla_norm_w=gla_norm_w, w_out=w_out, ln2_g=ln2_g, ln2_b=ln2_b,
             ffn2_w_gate=ffn2_w_gate, ffn2_w_up=ffn2_w_up, ffn2_w_down=ffn2_w_down, ln3_g=ln3_g, ln3_b=ln3_b)
    bsz, l, _ = x.shape
    for i in range(ffn1_w_gate.shape[0]):
        x = _layer(x, p, i, tm=min(FFN_TOKENS, bsz * l), lb_gdn=min(GDN_BLOCK, l), lb_gla=min(GLA_BLOCK, l))
    return x
```

```python
import functools

import numpy as np
import jax
import jax.numpy as jnp
from jax import lax
from jax.experimental import pallas as pl
from jax.experimental.pallas import tpu as pltpu

GDN_HEADS = 4
GDN_DK = 128
GDN_DV = 128
CONV_K = 4
GLA_HEADS = 4
GLA_DK = 64
GLA_DV = 128
GLA_RANK = 16
GLA_NORMALIZER = 16.0
CHUNK = 64
DEPTH = 1
ALPHA = (2.0 * DEPTH) ** 0.25
LN_EPS = 1e-5
RMS_EPS = 1e-6

GDN_QK_W = GDN_HEADS * GDN_DK
GDN_V_W = GDN_HEADS * GDN_DV
GLA_QK_W = GLA_HEADS * GLA_DK
GLA_V_W = GLA_HEADS * GLA_DV
CONV_CH = 2 * GDN_QK_W + GDN_V_W

LANES = 128
SUBLANES = 8
MXU_DIM = 256
VMEM_LIMIT_BYTES = 56 * 1024 * 1024

FFN_TOKENS = 1024
GDN_BLOCK = 256
GLA_BLOCK = 256

SMALL_W = LANES
PROJ_W = CONV_CH + GDN_V_W + 2 * GLA_QK_W + 2 * GLA_V_W + SMALL_W
OFF_QKV = 0
OFF_Z = CONV_CH
OFF_GQ = OFF_Z + GDN_V_W
OFF_GK = OFF_GQ + GLA_QK_W
OFF_GV = OFF_GK + GLA_QK_W
OFF_GG = OFF_GV + GLA_V_W
OFF_SMALL = OFF_GG + GLA_V_W

BF16 = jnp.bfloat16
F32 = jnp.float32


def _dot(a, b):
    return jnp.dot(a.astype(BF16), b.astype(BF16), preferred_element_type=F32)


def _dot_nt(a, b):
    return lax.dot_general(a.astype(BF16), b.astype(BF16), (((1,), (1,)), ((), ())),
                           preferred_element_type=F32)


def _dot_tn(a, b):
    return lax.dot_general(a.astype(BF16), b.astype(BF16), (((0,), (0,)), ((), ())),
                           preferred_element_type=F32)


def _split_bf16(x, pieces):
    out = []
    for _ in range(pieces - 1):
        p = x.astype(BF16)
        out.append(p)
        x = x - p.astype(F32)
    out.append(x.astype(BF16))
    return out


def _sum_dot(sel3, x):
    return jnp.dot(sel3, jnp.concatenate(_split_bf16(x, 3), axis=0), preferred_element_type=F32)


def _dot_split(a, b):
    a_hi, a_lo = _split_bf16(a, 2)
    b_hi, b_lo = _split_bf16(b, 2)
    return jnp.dot(jnp.concatenate([a_hi, a_lo, a_hi], axis=1), jnp.concatenate([b_hi, b_hi, b_lo], axis=0),
                   preferred_element_type=F32)


NEG_LOG2_E = -1.4426950408889634


def _sigmoid(x):
    return 1.0 / (1.0 + jnp.exp2(x * NEG_LOG2_E))


def _silu(x):
    return x * _sigmoid(x)


def _softplus(x):
    return jnp.maximum(x, 0.0) + jnp.log(1.0 + jnp.exp(-jnp.abs(x)))


def _layer_norm_rows(r, g, b):
    mu = jnp.mean(r, axis=-1, keepdims=True)
    c = r - mu
    var = jnp.mean(c * c, axis=-1, keepdims=True)
    return c * lax.rsqrt(var + LN_EPS) * g + b


ROW_SPLITS = 4


def _row_groups(tm):
    step = tm // ROW_SPLITS
    return [slice(r * step, (r + 1) * step) for r in range(ROW_SPLITS)]


def _swiglu_ln(x, wg_ref, wu_ref, wd_ref, g_ref, b_ref, act_ref, o_ref, ff_chunk):
    xb = x.astype(BF16)
    d_ff = wg_ref.shape[1]
    for j in range(d_ff // ff_chunk):
        sl = slice(j * ff_chunk, (j + 1) * ff_chunk)
        gate = jnp.dot(xb, wg_ref[:, sl], preferred_element_type=F32)
        up = jnp.dot(xb, wu_ref[:, sl], preferred_element_type=F32)
        act_ref[:, sl] = (_silu(gate) * up).astype(BF16)
    for rs in _row_groups(x.shape[0]):
        y = jnp.dot(act_ref[rs, :], wd_ref[...], preferred_element_type=F32)
        o_ref[rs, :] = _layer_norm_rows(ALPHA * x[rs, :] + 0.5 * y, g_ref[...], b_ref[...])


def _ffn_ln_kernel(x_ref, wg_ref, wu_ref, wd_ref, g_ref, b_ref, o_ref, act_ref, *, ff_chunk):
    _swiglu_ln(x_ref[...], wg_ref, wu_ref, wd_ref, g_ref, b_ref, act_ref, o_ref, ff_chunk)


def _mix_ffn_ln_kernel(h_ref, oa_ref, ob_ref, wa_ref, wb_ref, g2_ref, b2_ref,
                       wg_ref, wu_ref, wd_ref, g3_ref, b3_ref, o_ref, act_ref, *, ff_chunk):
    xs = []
    for rs in _row_groups(h_ref.shape[0]):
        mix = (jnp.dot(oa_ref[rs, :].astype(BF16), wa_ref[...], preferred_element_type=F32)
               + jnp.dot(ob_ref[rs, :].astype(BF16), wb_ref[...], preferred_element_type=F32))
        xs.append(_layer_norm_rows(ALPHA * h_ref[rs, :] + mix, g2_ref[...], b2_ref[...]))
    _swiglu_ln(jnp.concatenate(xs, axis=0), wg_ref, wu_ref, wd_ref, g3_ref, b3_ref, act_ref, o_ref, ff_chunk)


def _mix_ffn_ln(h2d, oa2d, ob2d, w_a, w_b, ln2_g, ln2_b, wg, wu, wd, ln3_g, ln3_b, *, tm):
    t, d = h2d.shape
    d_ff = wg.shape[1]
    const = lambda i: (0, 0)
    tile = lambda w: pl.BlockSpec((tm, w), lambda i: (i, 0))
    resident = lambda a: pl.BlockSpec(a.shape, const, pipeline_mode=pl.Buffered(1))
    return pl.pallas_call(
        functools.partial(_mix_ffn_ln_kernel, ff_chunk=MXU_DIM),
        grid=(t // tm,),
        in_specs=[tile(d), tile(oa2d.shape[1]), tile(ob2d.shape[1]), resident(w_a), resident(w_b),
                  pl.BlockSpec((1, d), const), pl.BlockSpec((1, d), const),
                  resident(wg), resident(wu), resident(wd),
                  pl.BlockSpec((1, d), const), pl.BlockSpec((1, d), const)],
        out_specs=tile(d),
        out_shape=jax.ShapeDtypeStruct((t, d), F32),
        scratch_shapes=[pltpu.VMEM((tm, d_ff), BF16)],
        compiler_params=pltpu.CompilerParams(
            dimension_semantics=("parallel",), vmem_limit_bytes=VMEM_LIMIT_BYTES),
        name="mix_ffn_ln",
    )(h2d, oa2d, ob2d, w_a, w_b, ln2_g, ln2_b, wg, wu, wd, ln3_g, ln3_b)


def _ffn_ln(x2d, wg, wu, wd, ln_g, ln_b, *, tm):
    t, d = x2d.shape
    d_ff = wg.shape[1]
    const = lambda i: (0, 0)
    return pl.pallas_call(
        functools.partial(_ffn_ln_kernel, ff_chunk=MXU_DIM),
        grid=(t // tm,),
        in_specs=[
            pl.BlockSpec((tm, d), lambda i: (i, 0)),
            pl.BlockSpec((d, d_ff), const, pipeline_mode=pl.Buffered(1)),
            pl.BlockSpec((d, d_ff), const, pipeline_mode=pl.Buffered(1)),
            pl.BlockSpec((d_ff, d), const, pipeline_mode=pl.Buffered(1)),
            pl.BlockSpec((1, d), const),
            pl.BlockSpec((1, d), const),
        ],
        out_specs=pl.BlockSpec((tm, d), lambda i: (i, 0)),
        out_shape=jax.ShapeDtypeStruct((t, d), F32),
        scratch_shapes=[pltpu.VMEM((tm, d_ff), BF16)],
        compiler_params=pltpu.CompilerParams(
            dimension_semantics=("parallel",), vmem_limit_bytes=VMEM_LIMIT_BYTES),
        name="ffn_ln",
    )(x2d, wg, wu, wd, ln_g, ln_b)


def _proj_kernel(h_ref, w_ref, o_ref):
    o_ref[...] = jnp.dot(h_ref[...].astype(BF16), w_ref[...], preferred_element_type=F32)


def _proj(h2d, w, *, tm):
    t, d = h2d.shape
    n = w.shape[1]
    return pl.pallas_call(
        _proj_kernel,
        grid=(t // tm,),
        in_specs=[
            pl.BlockSpec((tm, d), lambda i: (i, 0)),
            pl.BlockSpec((d, n), lambda i: (0, 0), pipeline_mode=pl.Buffered(1)),
        ],
        out_specs=pl.BlockSpec((tm, n), lambda i: (i, 0)),
        out_shape=jax.ShapeDtypeStruct((t, n), F32),
        compiler_params=pltpu.CompilerParams(
            dimension_semantics=("parallel",), vmem_limit_bytes=VMEM_LIMIT_BYTES),
        name="proj",
    )(h2d, w)


def _iota2(shape, dim):
    return lax.broadcasted_iota(jnp.int32, shape, dim)


def _gated_rmsnorm(o, gate, w):
    ms = jnp.mean(o * o, axis=-1, keepdims=True)
    return o * lax.rsqrt(ms + RMS_EPS) * w * _silu(gate)


GDN_ROWS_PER_STEP = 4


def _gdn_kernel(qkv_ref, z_ref, sm_ref, convw_ref, alog_ref, dtb_ref, nw_ref, tril_ref,
                o_ref, xpad_ref, act_ref, s_ref, wq_ref, u_ref, kd_ref, qk_ref, dl_ref, *, lb):
    c = CHUNK
    halo = SUBLANES
    n_chunks = lb // c
    n_rows = qkv_ref.shape[0]
    n_slabs = CONV_CH // LANES
    n_lanes = n_rows * GDN_HEADS
    lanes = [(r, h) for r in range(n_rows) for h in range(GDN_HEADS)]
    chains = [(ci, ln) for ci in range(n_chunks) for ln in range(n_lanes)]
    rows = lambda ci: slice(ci * c, (ci + 1) * c)
    nw = nw_ref[...]

    @pl.when(pl.program_id(1) == 0)
    def _():
        xpad_ref[:, 0:halo, :] = jnp.zeros((n_rows * n_slabs, halo, LANES), F32)
        for ref in (s_ref, wq_ref, u_ref, kd_ref, qk_ref, dl_ref):
            ref[...] = jnp.zeros(ref.shape, ref.dtype)

    state = [s_ref[ln] for ln in range(n_lanes)]
    for ci in range(n_chunks):
        idx = [ci * n_lanes + ln for ln in range(n_lanes)]
        ws = [_dot(wq_ref[i], state[ln]) for ln, i in enumerate(idx)]
        v_new = [u_ref[i] - ws[ln][:c] for ln, i in enumerate(idx)]
        upd = [_dot_tn(kd_ref[i], v_new[ln]) for ln, i in enumerate(idx)]
        intra = [_dot(qk_ref[i], v_new[ln]) for ln, i in enumerate(idx)]
        for ln, i in enumerate(idx):
            r, h = lanes[ln]
            state[ln] = state[ln] * dl_ref[i] + upd[ln]
            zg = z_ref[r, rows(ci), h * GDN_DV:(h + 1) * GDN_DV]
            o_ref[r, rows(ci), h * GDN_DV:(h + 1) * GDN_DV] = _gated_rmsnorm(ws[ln][c:] + intra[ln], zg, nw)
    for ln in range(n_lanes):
        s_ref[ln] = state[ln]

    pitch = (halo + lb) // SUBLANES
    assert pitch * SUBLANES == halo + lb and pitch % 8 != 0
    for r in range(n_rows):
        for j in range(n_slabs):
            sj = r * n_slabs + j
            ls = slice(j * LANES, (j + 1) * LANES)
            xpad_ref[sj, halo:halo + lb, :] = qkv_ref[r, :, ls]
            taps = [jnp.broadcast_to(convw_ref[k:k + 1, ls], (SUBLANES, LANES)) for k in range(CONV_K)]
            tiles = [xpad_ref[sj, pl.ds(g, SUBLANES, stride=pitch), :] for g in range(pitch)]
            wrapped = {-k: pltpu.roll(tiles[pitch - k], 1, 0) for k in range(1, CONV_K)}
            back = lambda g: tiles[g] if g >= 0 else wrapped[g]
            is_qk = j < 2 * GDN_HEADS
            scale = GDN_DK ** -0.5 if j < GDN_HEADS else 1.0
            for g in range(pitch):
                y = taps[CONV_K - 1] * tiles[g]
                for k in range(CONV_K - 1):
                    y = y + taps[k] * back(g - (CONV_K - 1 - k))
                y = _silu(y)
                if is_qk:
                    y = y * (lax.rsqrt(jnp.sum(y * y, axis=-1, keepdims=True) + RMS_EPS) * scale)
                act_ref[sj, pl.ds(g, SUBLANES, stride=pitch), :] = y
            xpad_ref[sj, 0:halo, :] = xpad_ref[sj, lb:lb + halo, :]

    row = _iota2((c, c), 0)
    col = _iota2((c, c), 1)
    causal = row >= col
    strict = row > col
    eye = (row == col).astype(F32)
    tril = tril_ref[...]

    beta_all, g_cum, g_cum_t = [], [], []
    for r in range(n_rows):
        sm = sm_ref[r]
        beta_all.append(_sigmoid(sm))
        g_all = -jnp.exp(alog_ref[...]) * _softplus(sm + dtb_ref[...])
        g_cum.append([_sum_dot(tril, g_all[rows(ci), :]) for ci in range(n_chunks)])
        g_cum_t.append([jnp.transpose(jnp.concatenate([g, g], axis=0))[:, :c] for g in g_cum[r]])

    qs, ks, vs, betas, gcs, decays, e_gs, g_lasts = {}, {}, {}, {}, {}, {}, {}, {}
    for ci, ln in chains:
        r, h = lanes[ln]
        rs = rows(ci)
        ars = slice(halo + ci * c, halo + (ci + 1) * c)
        qs[ci, ln] = act_ref[r * n_slabs + h, ars, :]
        ks[ci, ln] = act_ref[r * n_slabs + GDN_HEADS + h, ars, :]
        vs[ci, ln] = act_ref[r * n_slabs + 2 * GDN_HEADS + h, ars, :]
        betas[ci, ln] = beta_all[r][rs, h:h + 1]
        gc = g_cum[r][ci][:, GDN_HEADS + h:GDN_HEADS + h + 1]
        gr = g_cum_t[r][ci][GDN_HEADS + h:GDN_HEADS + h + 1, :]
        gcs[ci, ln] = gc
        g_lasts[ci, ln] = gc[c - 1:c, :]
        decays[ci, ln] = jnp.where(causal, jnp.exp(jnp.where(causal, gc - gr, 0.0)), 0.0)
        e_gs[ci, ln] = jnp.exp(gc)

    qk_kk = {ch: _dot_nt(jnp.concatenate([qs[ch], ks[ch]], axis=0), ks[ch]) for ch in chains}
    qk = {ch: qk_kk[ch][:c] * decays[ch] for ch in chains}
    a_neg = {ch: jnp.where(strict, -(betas[ch] * qk_kk[ch][c:] * decays[ch]), 0.0) for ch in chains}
    a_bf = {ch: a_neg[ch].astype(BF16) for ch in chains}
    n_pow = {ch: _dot(a_bf[ch], a_bf[ch]).astype(BF16) for ch in chains}
    t_inv = {ch: eye + a_neg[ch] for ch in chains}
    for _ in range(4):
        y = {ch: _dot(jnp.concatenate([n_pow[ch], t_inv[ch].astype(BF16)], axis=0), n_pow[ch]) for ch in chains}
        n_pow = {ch: y[ch][:c].astype(BF16) for ch in chains}
        t_inv = {ch: t_inv[ch] + y[ch][c:] for ch in chains}
    corr = {ch: _dot(t_inv[ch], n_pow[ch]) for ch in chains}
    t_inv = {ch: t_inv[ch] + corr[ch] for ch in chains}

    sol = {ch: _dot(t_inv[ch], jnp.concatenate(
        [vs[ch] * betas[ch], ks[ch] * (betas[ch] * e_gs[ch])], axis=1)) for ch in chains}
    wq = {ch: jnp.concatenate([sol[ch][:, GDN_DV:], qs[ch] * e_gs[ch]], axis=0) for ch in chains}
    k_dec = {ch: ks[ch] * jnp.exp(g_lasts[ch] - gcs[ch]) for ch in chains}

    for ci, ln in chains:
        i = ci * n_lanes + ln
        wq_ref[i] = wq[ci, ln].astype(BF16)
        u_ref[i] = sol[ci, ln][:, :GDN_DV]
        kd_ref[i] = k_dec[ci, ln].astype(BF16)
        qk_ref[i] = qk[ci, ln].astype(BF16)
        dl_ref[i] = jnp.broadcast_to(jnp.exp(g_lasts[ci, ln]), (1, GDN_DV))


def _gdn(proj3d, conv_w, alog_row, dtb_row, norm_w, tril, *, lb):
    b, l, _ = proj3d.shape
    nb = l // lb
    nr = GDN_ROWS_PER_STEP if b % GDN_ROWS_PER_STEP == 0 else 1
    n_chains = (lb // CHUNK) * GDN_HEADS * nr
    n_slabs = CONV_CH // LANES
    const2 = lambda bi, ti: (0, 0)
    prep = lambda ti: jnp.minimum(ti, nb - 1)
    scan = lambda ti: jnp.maximum(ti - 1, 0)
    return pl.pallas_call(
        functools.partial(_gdn_kernel, lb=lb),
        grid=(b // nr, nb + 1),
        in_specs=[
            pl.BlockSpec((nr, lb, CONV_CH), lambda bi, ti: (bi, prep(ti), OFF_QKV // CONV_CH)),
            pl.BlockSpec((nr, lb, GDN_V_W), lambda bi, ti: (bi, scan(ti), OFF_Z // GDN_V_W)),
            pl.BlockSpec((nr, lb, SMALL_W), lambda bi, ti: (bi, prep(ti), OFF_SMALL // SMALL_W)),
            pl.BlockSpec((CONV_K, CONV_CH), const2),
            pl.BlockSpec((1, SMALL_W), const2),
            pl.BlockSpec((1, SMALL_W), const2),
            pl.BlockSpec((1, GDN_DV), const2),
            pl.BlockSpec((CHUNK, 3 * CHUNK), const2),
        ],
        out_specs=pl.BlockSpec((nr, lb, GDN_V_W), lambda bi, ti: (bi, scan(ti), 0)),
        out_shape=jax.ShapeDtypeStruct((b, l, GDN_V_W), F32),
        scratch_shapes=[
            pltpu.VMEM((nr * n_slabs, lb + SUBLANES, LANES), F32),
            pltpu.VMEM((nr * n_slabs, lb + SUBLANES, LANES), F32),
            pltpu.VMEM((nr * GDN_HEADS, GDN_DK, GDN_DV), F32),
            pltpu.VMEM((n_chains, 2 * CHUNK, GDN_DK), BF16),
            pltpu.VMEM((n_chains, CHUNK, GDN_DV), F32),
            pltpu.VMEM((n_chains, CHUNK, GDN_DK), BF16),
            pltpu.VMEM((n_chains, CHUNK, CHUNK), BF16),
            pltpu.VMEM((n_chains, 1, GDN_DV), F32),
        ],
        compiler_params=pltpu.CompilerParams(
            dimension_semantics=("parallel", "arbitrary"), vmem_limit_bytes=VMEM_LIMIT_BYTES),
        name="gdn",
    )(proj3d, proj3d, proj3d, conv_w, alog_row, dtb_row, norm_w, tril)


GLA_LEVELS = (32, 16, 8, 4, 2, 1)
GLA_ROWS_PER_STEP = 4


def _gla_sum_matrix():
    c = CHUNK
    r = np.arange(c)[:, None]
    s = np.arange(c)[None, :]
    blocks = [(s <= r), (s > r)]
    for hs in GLA_LEVELS:
        pos = r % (2 * hs)
        m = r - pos + hs
        later = pos >= hs
        blocks.append(np.where(later, (s > m) & (s <= r), (s > r) & (s <= m)))
    return np.concatenate(blocks, axis=0).astype(np.float32)


def _gla_kernel(q_ref, k_ref, v_ref, g_ref, sm_ref, wgk_ref, bgk_ref, nw_ref, lmat_ref,
                o_ref, s_ref, *, lb):
    c = CHUNK

    @pl.when(pl.program_id(1) == 0)
    def _():
        s_ref[...] = jnp.zeros(s_ref.shape, F32)

    lmat = lmat_ref[...]
    nw = nw_ref[...]
    scale = GLA_DK ** -0.5

    n_rows = q_ref.shape[0]
    logg_rows = []
    for r in range(n_rows):
        gate_pre = _dot_split(sm_ref[r], wgk_ref[...]) + bgk_ref[...]
        logg_rows.append(
            (jnp.minimum(gate_pre, 0.0) - jnp.log(1.0 + jnp.exp(-jnp.abs(gate_pre)))) / GLA_NORMALIZER)

    per_row = lb // c
    n_chunks = n_rows * per_row
    chains = [(ci, h) for ci in range(n_chunks) for h in range(GLA_HEADS)]
    rw = lambda ci: ci // per_row
    rows = lambda ci: slice((ci % per_row) * c, (ci % per_row + 1) * c)
    lanes = lambda h: slice(h * GLA_DK, (h + 1) * GLA_DK)

    sums = [_sum_dot(lmat, logg_rows[rw(ci)][rows(ci), :]) for ci in range(n_chunks)]
    q_in, k_out, e_last, qm, km, qk_diag = [], [], [], [], [], []
    for ci in range(n_chunks):
        q = q_ref[rw(ci), rows(ci), :] * scale
        k = k_ref[rw(ci), rows(ci), :]
        e_b = jnp.exp(sums[ci][0:c])
        e_last.append(e_b[c - 1:c, :])
        q_in.append(q * e_b)
        k_out.append(k * jnp.exp(sums[ci][c:2 * c]))
        qm_c, km_c = [], []
        for li, hs in enumerate(GLA_LEVELS):
            e_l = jnp.exp(sums[ci][(2 + li) * c:(3 + li) * c])
            qm_c.append(q * e_l)
            km_c.append(k * e_l)
        qm.append(qm_c)
        km.append(km_c)
        qk_diag.append(q * k)

    pair_w = 2 * GLA_DK
    pairs = [(ci, p) for ci in range(n_chunks) for p in range(GLA_HEADS // 2)]
    pair_lanes = lambda p: slice(p * pair_w, (p + 1) * pair_w)
    row2 = _iota2((c, pair_w), 0)
    lane2 = _iota2((c, pair_w), 1)
    first = lane2 < GLA_DK
    col2 = lane2 & (GLA_DK - 1)
    eye2 = row2 == col2

    def pair_keys(x):
        return jnp.concatenate([jnp.where(first, x, 0.0), jnp.where(first, 0.0, x)], axis=0)

    att = {}
    for ci, p in pairs:
        qk_p = qk_diag[ci][:, pair_lanes(p)]
        s_a = jnp.sum(jnp.where(first, qk_p, 0.0), axis=-1, keepdims=True)
        s_b = jnp.sum(jnp.where(first, 0.0, qk_p), axis=-1, keepdims=True)
        att[ci, p] = jnp.where(eye2, jnp.where(first, s_a, s_b), 0.0)
    for li, hs in enumerate(GLA_LEVELS):
        part = {(ci, p): _dot_nt(qm[ci][li][:, pair_lanes(p)], pair_keys(km[ci][li][:, pair_lanes(p)]))
                for ci, p in pairs}
        keep = ((row2 ^ col2) < 2 * hs) & ((row2 & hs) != 0) & ((col2 & hs) == 0)
        for cp in pairs:
            att[cp] = att[cp] + jnp.where(keep, part[cp], 0.0)
    vs = {(ci, h): v_ref[rw(ci), rows(ci), h * GLA_DV:(h + 1) * GLA_DV] for ci, h in chains}
    zero_v = jnp.zeros((c, GLA_DV), F32)
    intra_pair = {(ci, p): _dot(att[ci, p], jnp.concatenate(
        [jnp.concatenate([vs[ci, 2 * p], zero_v], axis=1),
         jnp.concatenate([zero_v, vs[ci, 2 * p + 1]], axis=1)], axis=0)) for ci, p in pairs}
    intra = {(ci, h): intra_pair[ci, h // 2][:, (h % 2) * GLA_DV:(h % 2 + 1) * GLA_DV] for ci, h in chains}
    upd ={(ci, h): _dot_tn(vs[ci, h], k_out[ci][:, lanes(h)]) for ci, h in chains}

    state = [s_ref[i] for i in range(n_rows * GLA_HEADS)]
    for ci in range(n_chunks):
        r = rw(ci)
        inter = [_dot_nt(q_in[ci][:, lanes(h)], state[r * GLA_HEADS + h]) for h in range(GLA_HEADS)]
        for h in range(GLA_HEADS):
            i = r * GLA_HEADS + h
            state[i] = state[i] * e_last[ci][:, lanes(h)] + upd[ci, h]
            gg = g_ref[r, rows(ci), h * GLA_DV:(h + 1) * GLA_DV]
            o_ref[r, rows(ci), h * GLA_DV:(h + 1) * GLA_DV] = _gated_rmsnorm(inter[h] + intra[ci, h], gg, nw)
    for i in range(n_rows * GLA_HEADS):
        s_ref[i] = state[i]


def _gla(proj3d, wgk_pad, bgk_row, norm_w, lmat, *, lb):
    b, l, _ = proj3d.shape
    nr = GLA_ROWS_PER_STEP if b % GLA_ROWS_PER_STEP == 0 else 1
    const2 = lambda bi, ti: (0, 0)
    return pl.pallas_call(
        functools.partial(_gla_kernel, lb=lb),
        grid=(b // nr, l // lb),
        in_specs=[
            pl.BlockSpec((nr, lb, GLA_QK_W), lambda bi, ti: (bi, ti, OFF_GQ // GLA_QK_W)),
            pl.BlockSpec((nr, lb, GLA_QK_W), lambda bi, ti: (bi, ti, OFF_GK // GLA_QK_W)),
            pl.BlockSpec((nr, lb, GLA_V_W), lambda bi, ti: (bi, ti, OFF_GV // GLA_V_W)),
            pl.BlockSpec((nr, lb, GLA_V_W), lambda bi, ti: (bi, ti, OFF_GG // GLA_V_W)),
            pl.BlockSpec((nr, lb, SMALL_W), lambda bi, ti: (bi, ti, OFF_SMALL // SMALL_W)),
            pl.BlockSpec((SMALL_W, GLA_QK_W), const2),
            pl.BlockSpec((1, GLA_QK_W), const2),
            pl.BlockSpec((1, GLA_DV), const2),
            pl.BlockSpec(lmat.shape, const2),
        ],
        out_specs=pl.BlockSpec((nr, lb, GLA_V_W), lambda bi, ti: (bi, ti, 0)),
        out_shape=jax.ShapeDtypeStruct((b, l, GLA_V_W), F32),
        scratch_shapes=[pltpu.VMEM((nr * GLA_HEADS, GLA_DV, GLA_DK), F32)],
        compiler_params=pltpu.CompilerParams(
            dimension_semantics=("parallel", "arbitrary"), vmem_limit_bytes=VMEM_LIMIT_BYTES),
        name="gla",
    )(proj3d, proj3d, proj3d, proj3d, proj3d, wgk_pad, bgk_row, norm_w, lmat)


def _regroup_w_in(w_in):
    d = w_in.shape[0]
    o = 0
    qkv = w_in[:, o:o + CONV_CH]; o += CONV_CH
    z = w_in[:, o:o + GDN_V_W]; o += GDN_V_W
    beta = w_in[:, o:o + GDN_HEADS]; o += GDN_HEADS
    dec = w_in[:, o:o + GDN_HEADS]; o += GDN_HEADS
    gq = w_in[:, o:o + GLA_QK_W]; o += GLA_QK_W
    gk = w_in[:, o:o + GLA_QK_W]; o += GLA_QK_W
    gv = w_in[:, o:o + GLA_V_W]; o += GLA_V_W
    gg = w_in[:, o:o + GLA_V_W]; o += GLA_V_W
    lr = w_in[:, o:o + GLA_RANK]
    small = jnp.concatenate(
        [beta, dec, lr, jnp.zeros((d, SMALL_W - 2 * GDN_HEADS - GLA_RANK), w_in.dtype)], axis=1)
    return jnp.concatenate([qkv, z, gq, gk, gv, gg, small], axis=1)


def _lane_row(vals, offset, width):
    row = jnp.zeros((1, width), F32)
    return row.at[0, offset:offset + vals.shape[0]].set(vals.astype(F32))


def _layer(x, p, i, *, tm, lb_gdn, lb_gla):
    bsz, l, d = x.shape
    t = bsz * l
    row = lambda a: a.reshape(1, -1).astype(F32)

    h = _ffn_ln(x.reshape(t, d), p["ffn1_w_gate"][i].astype(BF16), p["ffn1_w_up"][i].astype(BF16),
                p["ffn1_w_down"][i].astype(BF16), row(p["ln1_g"][i]), row(p["ln1_b"][i]), tm=tm)

    proj = _proj(h, _regroup_w_in(p["w_in"][i].astype(BF16)), tm=tm).reshape(bsz, l, PROJ_W)

    tril = jnp.asarray(np.tile(np.tril(np.ones((CHUNK, CHUNK), np.float32)), (1, 3)), BF16)
    o_a = _gdn(proj, p["conv_w"][i].astype(F32),
               _lane_row(p["a_log"][i], GDN_HEADS, SMALL_W), _lane_row(p["dt_bias"][i], GDN_HEADS, SMALL_W),
               row(p["gdn_norm_w"][i]), tril, lb=lb_gdn)

    wgk_pad = jnp.zeros((SMALL_W, GLA_QK_W), F32).at[2 * GDN_HEADS:2 * GDN_HEADS + GLA_RANK, :].set(
        p["w_gk"][i].astype(F32))
    o_b = _gla(proj, wgk_pad, row(p["b_gk"][i]), row(p["gla_norm_w"][i]),
               jnp.asarray(np.tile(_gla_sum_matrix(), (1, 3)), BF16), lb=lb_gla)

    w_out = p["w_out"][i].astype(BF16)
    out = _mix_ffn_ln(h, o_a.reshape(t, GDN_V_W), o_b.reshape(t, GLA_V_W), w_out[:GDN_V_W], w_out[GDN_V_W:],
                      row(p["ln2_g"][i]), row(p["ln2_b"][i]),
                      p["ffn2_w_gate"][i].astype(BF16), p["ffn2_w_up"][i].astype(BF16),
                      p["ffn2_w_down"][i].astype(BF16), row(p["ln3_g"][i]), row(p["ln3_b"][i]), tm=tm)
    return out.reshape(bsz, l, d)


def kernel(x, ffn1_w_gate, ffn1_w_up, ffn1_w_down, ln1_g, ln1_b, w_in, conv_w, a_log, dt_bias, gdn_norm_w,
           w_gk, b_gk, gla_norm_w, w_out, ln2_g, ln2_b, ffn2_w_gate, ffn2_w_up, ffn2_w_down, ln3_g, ln3_b):
    p = dict(ffn1_w_gate=ffn1_w_gate, ffn1_w_up=ffn1_w_up, ffn1_w_down=ffn1_w_down, ln1_g=ln1_g, ln1_b=ln1_b,
             w_in=w_in, conv_w=conv_w, a_log=a_log, dt_bias=dt_bias, gdn_norm_w=gdn_norm_w,
             w_gk=w_gk, b_gk=b_gk, gla_norm_w=gla_norm_w, w_out=w_out, ln2_g=ln2_g, ln2_b=ln2_b,
             ffn2_w_gate=ffn2_w_gate, ffn2_w_up=ffn2_w_up, ffn2_w_down=ffn2_w_down, ln3_g=ln3_g, ln3_b=ln3_b)
    bsz, l, _ = x.shape
    for i in range(ffn1_w_gate.shape[0]):
        x = _layer(x, p, i, tm=min(FFN_TOKENS, bsz * l), lb_gdn=min(GDN_BLOCK, l), lb_gla=min(GLA_BLOCK, l))
    return x
```
